```python
import numpy as np
import jax, jax.numpy as jnp
from jax import lax

D_MODEL = 2048
BATCH = 4
SEQ = 8192
DEPTH = 2

CHUNK = 64
D_MIX = D_MODEL
D_A = D_MIX // 2
D_B = D_MIX // 4
D_C = D_MIX - D_A - D_B
CONV_A = 3
GMLP_BLOCK = 128
HEADS_B = 4
HEAD_DIM_B = D_B // HEADS_B
CONV_C = 31
D_IN = 3 * D_A + 2 * D_B + 2 * D_C
N_EXPERTS = 32
N_GROUPS = 8
EXPERTS_PER_GROUP = N_EXPERTS // N_GROUPS
TOP_K = 2
D_EXPERT = D_MODEL // 4
MOE_BLOCK = 256
LN_EPS = 1e-5
DEEPNORM_ALPHA = (2 * DEPTH) ** 0.25
DEEPNORM_BETA = (8 * DEPTH) ** -0.25

kernel_name = "hybrid_conv_gmlp_conformer_grouped_moe_deepnorm"


def layer_norm(x, g, b):
    xf = x.astype(jnp.float32)
    mu = jnp.mean(xf, axis=-1, keepdims=True)
    var = jnp.mean(jnp.square(xf - mu), axis=-1, keepdims=True)
    y = (xf - mu) * lax.rsqrt(var + LN_EPS) * g.astype(jnp.float32) + b.astype(jnp.float32)
    return y.astype(x.dtype)


def causal_dwconv(x, w):
    k, c = w.shape
    return lax.conv_general_dilated(
        x, w[:, None, :].astype(x.dtype), window_strides=(1,), padding=[(k - 1, 0)],
        dimension_numbers=("NWC", "WIO", "NWC"), feature_group_count=c)


def spatial_mask(dtype):
    pos = np.arange(GMLP_BLOCK) // CHUNK
    return jnp.asarray(pos[None, :] <= pos[:, None], dtype=dtype)


def hybrid_mixer(x, w_in, conv_a, gate_ws, gate_bs, ln_v_g, ln_v_b,
                 conv_c, conv_c_b, ln_c_g, ln_c_b, w_out):
    bsz, seq, _ = x.shape
    h = jnp.einsum("bsd,de->bse", x, w_in)
    cuts = [int(c) for c in np.cumsum([D_A, D_A, D_A, D_B, D_B, D_C])]
    b_gate, c_gate, xa, u, v, ca, cg = jnp.split(h, cuts, axis=-1)
    y_a = b_gate * causal_dwconv(c_gate * xa, conv_a)
    vn = layer_norm(v, ln_v_g, ln_v_b)
    vb = vn.reshape(bsz, seq // GMLP_BLOCK, GMLP_BLOCK, HEADS_B, HEAD_DIM_B)
    ws = gate_ws * spatial_mask(gate_ws.dtype)[None]
    z = jnp.einsum("hij,bnjhd->bnihd", ws, vb) + gate_bs.T[None, None, :, :, None]
    y_b = u * z.reshape(bsz, seq, D_B)
    glu = ca * jax.nn.sigmoid(cg)
    y_c = causal_dwconv(glu, conv_c) + conv_c_b
    y_c = jax.nn.silu(layer_norm(y_c, ln_c_g, ln_c_b))
    y = jnp.concatenate([y_a, y_b, y_c], axis=-1)
    return jnp.einsum("bse,ed->bsd", y, w_out)


def grouped_moe(x, w_router, b_router, w1, w3, w2):
    bsz, seq, d = x.shape
    xt = x.reshape(-1, d)
    t = xt.shape[0]
    s = jax.nn.sigmoid(jnp.einsum("td,de->te", xt.astype(jnp.float32),
                                  w_router.astype(jnp.float32)))
    sel = (s + b_router.astype(jnp.float32)).reshape(t, N_GROUPS, EXPERTS_PER_GROUP)
    group_score = lax.top_k(sel, TOP_K)[0].sum(-1)
    g_idx = jnp.argmax(group_score, axis=-1)
    cand = jnp.take_along_axis(sel, g_idx[:, None, None], axis=1)[:, 0]
    _, local = lax.top_k(cand, TOP_K)
    e_idx = g_idx[:, None] * EXPERTS_PER_GROUP + local
    s_sel = jnp.take_along_axis(s, e_idx, axis=1)
    gates = s_sel / jnp.sum(s_sel, axis=-1, keepdims=True)
    m = t * TOP_K
    flat_e = e_idx.reshape(m)
    flat_t = jnp.arange(m, dtype=jnp.int32) // TOP_K
    flat_g = gates.reshape(m)
    order = jnp.argsort(flat_e)
    se = flat_e[order]
    counts = jnp.bincount(flat_e, length=N_EXPERTS)
    padded = (counts + MOE_BLOCK - 1) // MOE_BLOCK * MOE_BLOCK
    pend = jnp.cumsum(padded)
    pstart = pend - padded
    start = jnp.cumsum(counts) - counts
    dest = pstart[se] + jnp.arange(m) - start[se]
    n_blocks = -(-m // MOE_BLOCK) + N_EXPERTS
    p = n_blocks * MOE_BLOCK
    slot_t = jnp.full((p,), t, jnp.int32).at[dest].set(flat_t[order])
    slot_g = jnp.zeros((p,), jnp.float32).at[dest].set(flat_g[order])
    blk_e = jnp.minimum(jnp.searchsorted(pend, jnp.arange(n_blocks) * MOE_BLOCK, side="right"),
                        N_EXPERTS - 1)
    x_pad = jnp.concatenate([xt, jnp.zeros((1, d), xt.dtype)], axis=0)

    def expert_block(args):
        tok, e = args
        xb = x_pad[tok]
        hb = jax.nn.silu(xb @ w1[e]) * (xb @ w3[e])
        return hb @ w2[e]

    y_slots = lax.map(expert_block, (slot_t.reshape(n_blocks, MOE_BLOCK), blk_e))
    y_slots = y_slots.reshape(p, d) * slot_g[:, None].astype(y_slots.dtype)
    y = jax.ops.segment_sum(y_slots, slot_t, num_segments=t + 1)[:t]
    return y.reshape(bsz, seq, d)


def setup_inputs(seed: int = 0) -> dict:
    key = jax.random.key(seed)
    ks = jax.random.split(key, 24)
    L, D, E, F = DEPTH, D_MODEL, N_EXPERTS, D_EXPERT
    nrm = lambda k, shape, scale: jax.random.normal(k, shape, jnp.float32) * scale
    return {
        "x": nrm(ks[0], (BATCH, SEQ, D), 1.0),
        "ln_in_g": 1.0 + nrm(ks[1], (D,), 0.02),
        "ln_in_b": nrm(ks[2], (D,), 0.02),
        "w_in": nrm(ks[3], (L, D, D_IN), D ** -0.5),
        "conv_a": nrm(ks[4], (L, CONV_A, D_A), CONV_A ** -0.5),
        "gate_ws": nrm(ks[5], (L, HEADS_B, GMLP_BLOCK, GMLP_BLOCK), GMLP_BLOCK ** -0.5),
        "gate_bs": 1.0 + nrm(ks[6], (L, HEADS_B, GMLP_BLOCK), 0.1),
        "ln_v_g": 1.0 + nrm(ks[7], (L, D_B), 0.02),
        "ln_v_b": nrm(ks[8], (L, D_B), 0.02),
        "conv_c": nrm(ks[9], (L, CONV_C, D_C), CONV_C ** -0.5),
        "conv_c_b": nrm(ks[10], (L, D_C), 0.02),
        "ln_c_g": 1.0 + nrm(ks[11], (L, D_C), 0.02),
        "ln_c_b": nrm(ks[12], (L, D_C), 0.02),
        "w_out": nrm(ks[13], (L, D_MIX, D), DEEPNORM_BETA * D_MIX ** -0.5),
        "ln1_g": 1.0 + nrm(ks[14], (L, D), 0.02),
        "ln1_b": nrm(ks[15], (L, D), 0.02),
        "w_router": nrm(ks[16], (D, E), D ** -0.5),
        "b_router": nrm(ks[17], (E,), 0.01),
        "w1": nrm(ks[18], (L, E, D, F), D ** -0.5),
        "w3": nrm(ks[19], (L, E, D, F), D ** -0.5),
        "w2": nrm(ks[20], (L, E, F, D), DEEPNORM_BETA * F ** -0.5),
        "ln2_g": 1.0 + nrm(ks[21], (L, D), 0.02),
        "ln2_b": nrm(ks[22], (L, D), 0.02),
    }


def reference(x, ln_in_g, ln_in_b, w_in, conv_a, gate_ws, gate_bs, ln_v_g, ln_v_b,
              conv_c, conv_c_b, ln_c_g, ln_c_b, w_out, ln1_g, ln1_b,
              w_router, b_router, w1, w3, w2, ln2_g, ln2_b):
    h = layer_norm(x, ln_in_g, ln_in_b)
    for l in range(DEPTH):
        mix = hybrid_mixer(h, w_in[l], conv_a[l], gate_ws[l], gate_bs[l], ln_v_g[l], ln_v_b[l],
                           conv_c[l], conv_c_b[l], ln_c_g[l], ln_c_b[l], w_out[l])
        h = layer_norm(DEEPNORM_ALPHA * h + mix, ln1_g[l], ln1_b[l])
        ffn = grouped_moe(h, w_router, b_router, w1[l], w3[l], w2[l])
        h = layer_norm(DEEPNORM_ALPHA * h + ffn, ln2_g[l], ln2_b[l])
    return h
```

```python
import functools

import numpy as np
import jax
import jax.numpy as jnp
from jax import lax
from jax.experimental import pallas as pl
from jax.experimental.pallas import tpu as pltpu

F32 = jnp.float32
BF16 = jnp.bfloat16

LN_EPS = 1e-5
CHUNK = 64
N_GROUPS = 8
EPG = 4
PAIR_A = (0, 0, 0, 1, 1, 2)
PAIR_B = (1, 2, 3, 3, 2, 3)
N_PAIRS = len(PAIR_A)
N_CLASSES = N_GROUPS * N_PAIRS
CLASS_ROWS = 64
LANES = 128
GATE_LANES = 2 * LANES
HALO_A = 8
HALO_C = 32
VMEM_LIMIT = 56 * 1024 * 1024

TM_MIX = 256
BM_MOE = 256
TM_CMB = 256

_NT = (((1,), (1,)), ((), ()))


def _layer_norm(x, g, b):
    mu = jnp.mean(x, axis=-1, keepdims=True)
    xc = x - mu
    var = jnp.mean(xc * xc, axis=-1, keepdims=True)
    return xc * lax.rsqrt(var + LN_EPS) * g + b


def _sigmoid(x):
    return 1.0 / (1.0 + jnp.exp(-x))


def _to_rows(v, n):
    r = lax.broadcasted_iota(jnp.int32, (n, n), 0)
    c = lax.broadcasted_iota(jnp.int32, (n, n), 1)
    return jnp.sum(jnp.where(r == c, jnp.broadcast_to(v, (n, n)), 0.0), axis=-1, keepdims=True)


def _mixer_kernel(x_ref, lng_ref, lnb_ref, win_ref, conva_ref, ws_ref, gbt_ref, lnvg_ref, lnvb_ref,
                  convc_ref, convcb_ref, lncg_ref, lncb_ref, wout_ref, ln1g_ref, ln1b_ref,
                  wr_ref, br_ref,
                  hx_ref, cls_ref, rank_ref, cnt_ref,
                  pa_scr, glu_scr, y_scr, cnt_scr,
                  *, apply_ln_in, tiles_per_seq, alpha, d_a, d_b, d_c, heads, blk):
    tm, d = x_ref.shape
    hd = d_b // heads
    i = pl.program_id(0)

    @pl.when(i % tiles_per_seq == 0)
    def _():
        pa_scr[0:HALO_A, :] = jnp.zeros((HALO_A, d_a), F32)
        glu_scr[0:HALO_C, :] = jnp.zeros((HALO_C, d_c), F32)

    @pl.when(i == 0)
    def _():
        cnt_scr[...] = jnp.zeros(cnt_scr.shape, F32)

    x = x_ref[...]
    if apply_ln_in:
        x = _layer_norm(x, lng_ref[...], lnb_ref[...])
    xb = x.astype(BF16)

    h_a = jnp.dot(xb, win_ref[:, 0:3 * d_a], preferred_element_type=F32)
    p = h_a[:, d_a:2 * d_a] * h_a[:, 2 * d_a:3 * d_a]
    pa_scr[HALO_A:HALO_A + tm, :] = p
    ka = conva_ref.shape[0]
    conv = conva_ref[ka - 1:ka, :] * p
    for k in range(ka - 1):
        off = HALO_A - (ka - 1) + k
        conv = conv + conva_ref[k:k + 1, :] * pa_scr[off:off + tm, :]
    y_scr[:, 0:d_a] = (h_a[:, 0:d_a] * conv).astype(BF16)
    pa_scr[0:HALO_A, :] = pa_scr[tm:tm + HALO_A, :]

    c0 = 3 * d_a
    h_b = jnp.dot(xb, win_ref[:, c0:c0 + 2 * d_b], preferred_element_type=F32)
    u = h_b[:, 0:d_b]
    vn = _layer_norm(h_b[:, d_b:2 * d_b], lnvg_ref[...], lnvb_ref[...]).astype(BF16)
    ri = lax.broadcasted_iota(jnp.int32, (blk, blk), 0) // CHUNK
    ci = lax.broadcasted_iota(jnp.int32, (blk, blk), 1) // CHUNK
    for h in range(heads):
        ws_h = jnp.where(ci <= ri, ws_ref[h], 0.0).astype(BF16)
        bias_h = gbt_ref[:, h:h + 1]
        for n in range(tm // blk):
            rows = slice(n * blk, (n + 1) * blk)
            cols = slice(h * hd, (h + 1) * hd)
            z = jnp.dot(ws_h, vn[rows, cols], preferred_element_type=F32) + bias_h
            y_scr[rows, d_a + h * hd:d_a + (h + 1) * hd] = (u[rows, cols] * z).astype(BF16)

    c0 = 3 * d_a + 2 * d_b
    h_c = jnp.dot(xb, win_ref[:, c0:c0 + 2 * d_c], preferred_element_type=F32)
    glu = h_c[:, 0:d_c] * _sigmoid(h_c[:, d_c:2 * d_c])
    glu_scr[HALO_C:HALO_C + tm, :] = glu
    kc = convc_ref.shape[0]
    acc = convcb_ref[...] + convc_ref[kc - 1:kc, :] * glu
    for k in range(kc - 1):
        off = HALO_C - (kc - 1) + k
        acc = acc + convc_ref[k:k + 1, :] * glu_scr[off:off + tm, :]
    yc = _layer_norm(acc, lncg_ref[...], lncb_ref[...])
    y_scr[:, d_a + d_b:d_a + d_b + d_c] = (yc * _sigmoid(yc)).astype(BF16)
    glu_scr[0:HALO_C, :] = glu_scr[tm:tm + HALO_C, :]

    mix = jnp.dot(y_scr[...], wout_ref[...], preferred_element_type=F32)
    h1 = _layer_norm(alpha * x + mix, ln1g_ref[...], ln1b_ref[...])
    hx_ref[:, 0:d] = h1

    ne = N_GROUPS * EPG
    hi = h1.astype(BF16)
    lo = (h1 - hi.astype(F32)).astype(BF16)
    o1 = lax.dot_general(wr_ref[...], hi, _NT, preferred_element_type=F32)
    o2 = lax.dot_general(wr_ref[0:ne, :], lo, _NT, preferred_element_type=F32)
    s = _sigmoid(o1[0:ne, :] + o1[ne:2 * ne, :] + o2)
    sel = s + br_ref[...]
    a = [sel[j * N_GROUPS:(j + 1) * N_GROUPS, :] for j in range(EPG)]
    sj = [s[j * N_GROUPS:(j + 1) * N_GROUPS, :] for j in range(EPG)]
    m01, n01 = jnp.maximum(a[0], a[1]), jnp.minimum(a[0], a[1])
    m23, n23 = jnp.maximum(a[2], a[3]), jnp.minimum(a[2], a[3])
    gscore = jnp.maximum(m01, m23) + jnp.maximum(jnp.minimum(m01, m23), jnp.maximum(n01, n23))
    gi = lax.broadcasted_iota(jnp.int32, (N_GROUPS, tm), 0)
    gmax = jnp.max(gscore, axis=0, keepdims=True)
    g_idx = jnp.min(jnp.where(gscore == gmax, gi, N_GROUPS), axis=0, keepdims=True)
    g_hot = gi == g_idx
    cand = [jnp.sum(jnp.where(g_hot, a[j], 0.0), axis=0, keepdims=True) for j in range(EPG)]
    scand = [jnp.sum(jnp.where(g_hot, sj[j], 0.0), axis=0, keepdims=True) for j in range(EPG)]

    def _first_argmax(vals):
        best, idx = vals[0], jnp.zeros((1, tm), jnp.int32)
        for j in range(1, EPG):
            better = vals[j] > best
            idx = jnp.where(better, j, idx)
            best = jnp.where(better, vals[j], best)
        return idx

    i1 = _first_argmax(cand)
    i2 = _first_argmax([jnp.where(i1 == j, -jnp.inf, cand[j]) for j in range(EPG)])

    def _pick(vals, idx):
        out = vals[0]
        for j in range(1, EPG):
            out = jnp.where(idx == j, vals[j], out)
        return out

    s1, s2 = _pick(scand, i1), _pick(scand, i2)
    den = s1 + s2
    gate1, gate2 = s1 / den, s2 / den
    first_is_a = i1 < i2
    ea, eb = jnp.minimum(i1, i2), jnp.maximum(i1, i2)
    g_a = jnp.where(first_is_a, gate1, gate2)
    g_b = jnp.where(first_is_a, gate2, gate1)
    pair = jnp.zeros((1, tm), jnp.int32)
    for q in range(N_PAIRS):
        pair = jnp.where((ea == PAIR_A[q]) & (eb == PAIR_B[q]), q, pair)
    cls = g_idx * N_PAIRS + pair
    cls_ref[...] = cls

    ki = lax.broadcasted_iota(jnp.int32, (CLASS_ROWS, tm), 0)
    c_hot = ki == cls
    onehot = jnp.where(c_hot, 1.0, 0.0)
    tr = lax.broadcasted_iota(jnp.int32, (tm, tm), 0)
    tc = lax.broadcasted_iota(jnp.int32, (tm, tm), 1)
    upper = jnp.where(tr < tc, 1.0, 0.0).astype(BF16)
    prefix = jnp.dot(onehot.astype(BF16), upper, preferred_element_type=F32)
    base = cnt_scr[:, 0:1]
    rank = jnp.sum(jnp.where(c_hot, prefix + base, 0.0), axis=0, keepdims=True)
    rank_ref[...] = rank.astype(jnp.int32)
    cnt_scr[...] = cnt_scr[...] + jnp.sum(onehot, axis=-1, keepdims=True)
    cnt_ref[...] = cnt_scr[...]

    hx_ref[:, d:d + LANES] = jnp.broadcast_to(_to_rows(g_a, tm), (tm, LANES))
    hx_ref[:, d + LANES:d + 2 * LANES] = jnp.broadcast_to(_to_rows(g_b, tm), (tm, LANES))


def _const_spec(shape, single_buffer=False):
    nd = len(shape)
    kw = {"pipeline_mode": pl.Buffered(1)} if single_buffer else {}
    return pl.BlockSpec(shape, lambda i, _nd=nd: (0,) * _nd, **kw)


def _mixer_call(x, p, *, apply_ln_in, seq, alpha):
    t, d = x.shape
    tm = TM_MIX
    d_in = p["w_in"].shape[1]
    d_a, d_b, d_c = p["conv_a"].shape[1], p["ln_v_g"].shape[1], p["conv_c"].shape[1]
    heads, blk = p["gate_ws"].shape[0], p["gate_ws"].shape[1]
    assert t % tm == 0 and seq % tm == 0 and tm % blk == 0 and 3 * d_a + 2 * d_b + 2 * d_c == d_in
    n_tiles = t // tm
    kern = functools.partial(
        _mixer_kernel, apply_ln_in=apply_ln_in, tiles_per_seq=seq // tm, alpha=alpha,
        d_a=d_a, d_b=d_b, d_c=d_c, heads=heads, blk=blk)
    consts = [p["ln_in_g"], p["ln_in_b"], p["w_in"], p["conv_a"], p["gate_ws"], p["gate_bs_t"],
              p["ln_v_g"], p["ln_v_b"], p["conv_c"], p["conv_c_b"], p["ln_c_g"], p["ln_c_b"],
              p["w_out"], p["ln1_g"], p["ln1_b"], p["w_router"], p["b_router"]]
    big = {2, 12}
    in_specs = [pl.BlockSpec((tm, d), lambda i: (i, 0))]
    in_specs += [_const_spec(c.shape, single_buffer=(k in big)) for k, c in enumerate(consts)]
    dx = d + GATE_LANES
    out_shape = (jax.ShapeDtypeStruct((t, dx), F32),
                 jax.ShapeDtypeStruct((1, t), jnp.int32),
                 jax.ShapeDtypeStruct((1, t), jnp.int32),
                 jax.ShapeDtypeStruct((CLASS_ROWS, LANES), F32))
    out_specs = (pl.BlockSpec((tm, dx), lambda i: (i, 0)),
                 pl.BlockSpec((1, tm), lambda i: (0, i)),
                 pl.BlockSpec((1, tm), lambda i: (0, i)),
                 pl.BlockSpec((CLASS_ROWS, LANES), lambda i: (0, 0)))
    scratch = [pltpu.VMEM((tm + HALO_A, d_a), F32),
               pltpu.VMEM((tm + HALO_C, d_c), F32),
               pltpu.VMEM((tm, d_a + d_b + d_c), BF16),
               pltpu.VMEM((CLASS_ROWS, LANES), F32)]
    return pl.pallas_call(
        kern, grid=(n_tiles,), in_specs=in_specs, out_specs=out_specs, out_shape=out_shape,
        scratch_shapes=scratch, name="mixer",
        compiler_params=pltpu.CompilerParams(
            dimension_semantics=("arbitrary",), vmem_limit_bytes=VMEM_LIMIT),
    )(x, *consts)


def _start_row_gather(idx_ref, src_hbm, buf, sem, slot, n_rows):
    def body(r, carry):
        t = idx_ref[0, 0, r]
        pltpu.make_async_copy(src_hbm.at[pl.ds(t, 1), :], buf.at[slot, pl.ds(r, 1), :],
                              sem.at[slot]).start()
        return carry
    lax.fori_loop(0, n_rows, body, 0)


def _wait_row_gather(src_hbm, buf, sem, slot, n_rows):
    pltpu.make_async_copy(src_hbm.at[pl.ds(0, n_rows), :], buf.at[slot], sem.at[slot]).wait()


def _experts_kernel(ba_ref, bb_ref, nv_ref, cur_idx_ref, nxt_idx_ref, hx_hbm,
                    w1a_ref, w3a_ref, w2a_ref, w1b_ref, w3b_ref, w2b_ref,
                    y_ref, xbuf, sem):
    bm, d = y_ref.shape
    f = w1a_ref.shape[1]
    i = pl.program_id(0)
    nv = nv_ref[0]
    slot = i % 2

    @pl.when(i == 0)
    def _():
        _start_row_gather(cur_idx_ref, hx_hbm, xbuf, sem, 0, bm)

    @pl.when(i + 1 < nv)
    def _():
        _start_row_gather(nxt_idx_ref, hx_hbm, xbuf, sem, 1 - slot, bm)

    @pl.when(i < nv)
    def _():
        _wait_row_gather(hx_hbm, xbuf, sem, slot, bm)
        xf = xbuf[slot]
        xb = xf[:, 0:d].astype(BF16)
        reps = f // LANES

        def hidden(w1_ref, w3_ref, gate):
            h1 = jnp.dot(xb, w1_ref[...], preferred_element_type=F32)
            h3 = jnp.dot(xb, w3_ref[...], preferred_element_type=F32)
            act = h1 * _sigmoid(h1) * h3
            return (act * jnp.concatenate([gate] * reps, axis=1)).astype(BF16)

        h_a = hidden(w1a_ref, w3a_ref, xf[:, d:d + LANES])
        h_b = hidden(w1b_ref, w3b_ref, xf[:, d + LANES:d + 2 * LANES])
        y_ref[...] = (jnp.dot(h_a, w2a_ref[...], preferred_element_type=F32)
                      + jnp.dot(h_b, w2b_ref[...], preferred_element_type=F32))

    @pl.when(i >= nv)
    def _():
        y_ref[...] = jnp.zeros((bm, d), F32)


def _experts_call(hx, slot_tok, blk_a, blk_b, nvalid, w1, w3, w2, d):
    bm = BM_MOE
    nb = slot_tok.shape[0]
    e, _, f = w1.shape
    dx = hx.shape[1]

    def wspec(shape, which):
        if which == 0:
            return pl.BlockSpec(shape, lambda i, ba, bb, nv: (ba[i], 0, 0))
        return pl.BlockSpec(shape, lambda i, ba, bb, nv: (bb[i], 0, 0))

    grid_spec = pltpu.PrefetchScalarGridSpec(
        num_scalar_prefetch=3,
        grid=(nb,),
        in_specs=[
            pl.BlockSpec((1, 1, bm), lambda i, ba, bb, nv: (i, 0, 0), memory_space=pltpu.SMEM),
            pl.BlockSpec((1, 1, bm), lambda i, ba, bb, nv: (jnp.minimum(i + 1, nb - 1), 0, 0),
                         memory_space=pltpu.SMEM),
            pl.BlockSpec(memory_space=pl.ANY),
            wspec((None, d, f), 0), wspec((None, d, f), 0), wspec((None, f, d), 0),
            wspec((None, d, f), 1), wspec((None, d, f), 1), wspec((None, f, d), 1),
        ],
        out_specs=pl.BlockSpec((bm, d), lambda i, ba, bb, nv: (i, 0)),
        scratch_shapes=[pltpu.VMEM((2, bm, dx), F32), pltpu.SemaphoreType.DMA((2,))],
    )
    return pl.pallas_call(
        _experts_kernel, grid_spec=grid_spec,
        out_shape=jax.ShapeDtypeStruct((nb * bm, d), F32), name="experts",
        compiler_params=pltpu.CompilerParams(
            dimension_semantics=("arbitrary",), vmem_limit_bytes=VMEM_LIMIT),
    )(blk_a, blk_b, nvalid, slot_tok, slot_tok, hx, w1, w3, w2, w1, w3, w2)


def _combine_kernel(cur_idx_ref, nxt_idx_ref, y_hbm, hx_ref, g_ref, b_ref, o_ref, ybuf, sem, *, alpha):
    tm, d = o_ref.shape
    i = pl.program_id(0)
    n = pl.num_programs(0)
    slot = i % 2

    @pl.when(i == 0)
    def _():
        _start_row_gather(cur_idx_ref, y_hbm, ybuf, sem, 0, tm)

    @pl.when(i + 1 < n)
    def _():
        _start_row_gather(nxt_idx_ref, y_hbm, ybuf, sem, 1 - slot, tm)

    _wait_row_gather(y_hbm, ybuf, sem, slot, tm)
    o_ref[...] = _layer_norm(alpha * hx_ref[:, 0:d] + ybuf[slot], g_ref[...], b_ref[...])


def _combine_call(y_sorted, dest, hx, g, b, d, alpha):
    tm = TM_CMB
    t, dx = hx.shape
    n_tiles = t // tm
    return pl.pallas_call(
        functools.partial(_combine_kernel, alpha=alpha),
        grid=(n_tiles,),
        in_specs=[
            pl.BlockSpec((1, 1, tm), lambda i: (i, 0, 0), memory_space=pltpu.SMEM),
            pl.BlockSpec((1, 1, tm), lambda i: (jnp.minimum(i + 1, n_tiles - 1), 0, 0),
                         memory_space=pltpu.SMEM),
            pl.BlockSpec(memory_space=pl.ANY),
            pl.BlockSpec((tm, dx), lambda i: (i, 0)),
            pl.BlockSpec((1, d), lambda i: (0, 0)),
            pl.BlockSpec((1, d), lambda i: (0, 0)),
        ],
        out_specs=pl.BlockSpec((tm, d), lambda i: (i, 0)),
        out_shape=jax.ShapeDtypeStruct((t, d), F32),
        scratch_shapes=[pltpu.VMEM((2, tm, d), F32), pltpu.SemaphoreType.DMA((2,))],
        name="combine",
        compiler_params=pltpu.CompilerParams(
            dimension_semantics=("arbitrary",), vmem_limit_bytes=VMEM_LIMIT),
    )(dest.reshape(n_tiles, 1, tm), dest.reshape(n_tiles, 1, tm), y_sorted, hx, g, b)


def _dispatch_plan(cls, rank, cnt, t):
    bm = BM_MOE
    nb = -(-t // bm) + N_CLASSES
    counts = cnt[:N_CLASSES, 0].astype(jnp.int32)
    nblk = (counts + bm - 1) // bm
    bend = jnp.cumsum(nblk)
    bstart = bend - nblk
    dest = bstart[cls] * bm + rank
    nvalid = bend[-1]
    blk = jnp.arange(nb, dtype=jnp.int32)
    blk_cls = jnp.searchsorted(bend, jnp.minimum(blk, nvalid - 1), side="right").astype(jnp.int32)
    grp, pair = blk_cls // N_PAIRS, blk_cls % N_PAIRS
    blk_a = grp * EPG + jnp.asarray(PAIR_A, jnp.int32)[pair]
    blk_b = grp * EPG + jnp.asarray(PAIR_B, jnp.int32)[pair]
    slot_tok = jnp.zeros((nb * bm,), jnp.int32).at[dest].set(jnp.arange(t, dtype=jnp.int32))
    return dest, slot_tok.reshape(nb, 1, bm), blk_a, blk_b, nvalid.reshape(1)


def kernel(x, ln_in_g, ln_in_b, w_in, conv_a, gate_ws, gate_bs, ln_v_g, ln_v_b, conv_c, conv_c_b,
           ln_c_g, ln_c_b, w_out, ln1_g, ln1_b, w_router, b_router, w1, w3, w2, ln2_g, ln2_b):
    bsz, seq, d = x.shape
    depth = w_in.shape[0]
    t = bsz * seq
    alpha = float((2 * depth) ** 0.25)
    ne = N_GROUPS * EPG
    assert w_router.shape[1] == ne and w1.shape[1] == ne

    perm = np.array([g * EPG + j for j in range(EPG) for g in range(N_GROUPS)])
    wr_t = w_router.astype(F32).T[perm]
    wr_hi = wr_t.astype(BF16)
    wr_lo = (wr_t - wr_hi.astype(F32)).astype(BF16)
    wr = jnp.concatenate([wr_hi, wr_lo], axis=0)
    br = b_router.astype(F32)[perm][:, None]

    row = lambda v: v.reshape(1, -1).astype(F32)
    h = x.reshape(t, d)
    for l in range(depth):
        p = {
            "ln_in_g": row(ln_in_g), "ln_in_b": row(ln_in_b),
            "w_in": w_in[l].astype(BF16), "conv_a": conv_a[l], "gate_ws": gate_ws[l],
            "gate_bs_t": gate_bs[l].T, "ln_v_g": row(ln_v_g[l]), "ln_v_b": row(ln_v_b[l]),
            "conv_c": conv_c[l], "conv_c_b": row(conv_c_b[l]), "ln_c_g": row(ln_c_g[l]),
            "ln_c_b": row(ln_c_b[l]), "w_out": w_out[l].astype(BF16),
            "ln1_g": row(ln1_g[l]), "ln1_b": row(ln1_b[l]), "w_router": wr, "b_router": br,
        }
        hx, cls, rank, cnt = _mixer_call(h, p, apply_ln_in=(l == 0), seq=seq, alpha=alpha)
        dest, slot_tok, blk_a, blk_b, nvalid = _dispatch_plan(cls[0], rank[0], cnt, t)
        y_sorted = _experts_call(hx, slot_tok, blk_a, blk_b, nvalid,
                                 w1[l].astype(BF16), w3[l].astype(BF16), w2[l].astype(BF16), d)
        h = _combine_call(y_sorted, dest, hx, row(ln2_g[l]), row(ln2_b[l]), d, alpha)
    return h.reshape(bsz, seq, d)
```

```python
import functools

import numpy as np
import jax
import jax.numpy as jnp
from jax import lax
from jax.experimental import pallas as pl
from jax.experimental.pallas import tpu as pltpu

F32 = jnp.float32
BF16 = jnp.bfloat16

LN_EPS = 1e-5
CHUNK = 64
N_GROUPS = 8
EPG = 4
PAIR_A = (0, 0, 0, 1, 1, 2)
PAIR_B = (1, 2, 3, 3, 2, 3)
N_PAIRS = len(PAIR_A)
N_CLASSES = N_GROUPS * N_PAIRS
CLASS_ROWS = 64
LANES = 128
SUBLANES = 8
HALO_A = 8
HALO_C = 32
VMEM_LIMIT = 56 * 1024 * 1024
DMA_UNROLL = 8
GATHER_SLOTS = 3

TM_MIX = 256
BM_MOE = 256
TM_CMB = 256

_NT = (((1,), (1,)), ((), ()))


def _layer_norm(x, g, b):
    mu = jnp.mean(x, axis=-1, keepdims=True)
    xc = x - mu
    var = jnp.mean(xc * xc, axis=-1, keepdims=True)
    return xc * lax.rsqrt(var + LN_EPS) * g + b


def _sigmoid(x):
    return 1.0 / (1.0 + jnp.exp(-x))


def _to_rows(v, n):
    r = lax.broadcasted_iota(jnp.int32, (n, n), 0)
    c = lax.broadcasted_iota(jnp.int32, (n, n), 1)
    return jnp.sum(jnp.where(r == c, jnp.broadcast_to(v, (n, n)), 0.0), axis=-1, keepdims=True)


def _slab_rows(ref, k, n, rows_per, slot=None):
    win = pl.ds(k, n, stride=rows_per)
    return ref.at[win, :] if slot is None else ref.at[slot, win, :]


def _mixer_kernel(x_ref, lng_ref, lnb_ref, win_ref, conva_ref, ws_ref, gbt_ref, lnvg_ref, lnvb_ref,
                  convc_ref, convcb_ref, lncg_ref, lncb_ref, wout_ref, ln1g_ref, ln1b_ref,
                  wr_ref, br_ref,
                  hx_ref, cls_ref, rank_ref, cnt_ref,
                  pa_scr, glu_scr, ya_scr, yb_scr, yc_scr, cnt_scr,
                  *, apply_ln_in, tiles_per_seq, alpha, d_a, d_b, d_c, heads, blk, x_rows):
    tm, d = x_ref.shape
    hd = d_b // heads
    i = pl.program_id(0)

    @pl.when(i % tiles_per_seq == 0)
    def _():
        pa_scr[0:HALO_A, :] = jnp.zeros((HALO_A, d_a), F32)
        glu_scr[:, 0:HALO_C, :] = jnp.zeros((d_c // LANES, HALO_C, LANES), F32)

    @pl.when(i == 0)
    def _():
        cnt_scr[...] = jnp.zeros(cnt_scr.shape, F32)

    x = x_ref[...]
    if apply_ln_in:
        x = _layer_norm(x, lng_ref[...], lnb_ref[...])
    xb = x.astype(BF16)

    c0 = 3 * d_a + 2 * d_b
    h_c = jnp.dot(xb, win_ref[:, c0:c0 + 2 * d_c], preferred_element_type=F32)
    glu = h_c[:, 0:d_c] * _sigmoid(h_c[:, d_c:2 * d_c])
    kc = convc_ref.shape[0]
    parts = []
    for j in range(d_c // LANES):
        ln = slice(j * LANES, (j + 1) * LANES)
        glu_scr[j, HALO_C:HALO_C + tm, :] = glu[:, ln]
        acc_j = convcb_ref[:, ln] + convc_ref[kc - 1:kc, ln] * glu[:, ln]
        for k in range(kc - 1):
            off = HALO_C - (kc - 1) + k
            acc_j = acc_j + convc_ref[k:k + 1, ln] * glu_scr[j, off:off + tm, :]
        parts.append(acc_j)
    acc = jnp.concatenate(parts, axis=1)
    yc = _layer_norm(acc, lncg_ref[...], lncb_ref[...])
    yc_scr[...] = (yc * _sigmoid(yc)).astype(BF16)

    h_a = jnp.dot(xb, win_ref[:, 0:3 * d_a], preferred_element_type=F32)
    p = h_a[:, d_a:2 * d_a] * h_a[:, 2 * d_a:3 * d_a]
    pa_scr[HALO_A:HALO_A + tm, :] = p
    ka = conva_ref.shape[0]
    conv = conva_ref[ka - 1:ka, :] * p
    for k in range(ka - 1):
        off = HALO_A - (ka - 1) + k
        conv = conv + conva_ref[k:k + 1, :] * pa_scr[off:off + tm, :]
    ya_scr[...] = (h_a[:, 0:d_a] * conv).astype(BF16)

    c0 = 3 * d_a
    h_b = jnp.dot(xb, win_ref[:, c0:c0 + 2 * d_b], preferred_element_type=F32)
    u = h_b[:, 0:d_b]
    vn = _layer_norm(h_b[:, d_b:2 * d_b], lnvg_ref[...], lnvb_ref[...]).astype(BF16)
    ri = lax.broadcasted_iota(jnp.int32, (blk, blk), 0) // CHUNK
    ci = lax.broadcasted_iota(jnp.int32, (blk, blk), 1) // CHUNK
    for h in range(heads):
        ws_h = jnp.where(ci <= ri, ws_ref[h], 0.0).astype(BF16)
        bias_h = gbt_ref[:, h:h + 1]
        for n in range(tm // blk):
            br = slice(n * blk, (n + 1) * blk)
            cols = slice(h * hd, (h + 1) * hd)
            z = jnp.dot(ws_h, vn[br, cols], preferred_element_type=F32) + bias_h
            yb_scr[br, cols] = (u[br, cols] * z).astype(BF16)

    mix = (jnp.dot(yc_scr[...], wout_ref[d_a + d_b:d_a + d_b + d_c, :], preferred_element_type=F32)
           + jnp.dot(ya_scr[...], wout_ref[0:d_a, :], preferred_element_type=F32)
           + jnp.dot(yb_scr[...], wout_ref[d_a:d_a + d_b, :], preferred_element_type=F32))
    h1 = _layer_norm(alpha * x + mix, ln1g_ref[...], ln1b_ref[...])
    for s in range(d // LANES):
        _slab_rows(hx_ref, s, tm, x_rows)[...] = h1[:, s * LANES:(s + 1) * LANES]

    ne = N_GROUPS * EPG
    hi = h1.astype(BF16)
    lo_part = (h1 - hi.astype(F32)).astype(BF16)
    o1 = lax.dot_general(wr_ref[...], hi, _NT, preferred_element_type=F32)
    o2 = lax.dot_general(wr_ref[0:ne, :], lo_part, _NT, preferred_element_type=F32)
    s = _sigmoid(o1[0:ne, :] + o1[ne:2 * ne, :] + o2)
    sel = s + br_ref[...]
    a = [sel[j * N_GROUPS:(j + 1) * N_GROUPS, :] for j in range(EPG)]
    sj = [s[j * N_GROUPS:(j + 1) * N_GROUPS, :] for j in range(EPG)]
    m01, n01 = jnp.maximum(a[0], a[1]), jnp.minimum(a[0], a[1])
    m23, n23 = jnp.maximum(a[2], a[3]), jnp.minimum(a[2], a[3])
    gscore = jnp.maximum(m01, m23) + jnp.maximum(jnp.minimum(m01, m23), jnp.maximum(n01, n23))
    gi = lax.broadcasted_iota(jnp.int32, (N_GROUPS, tm), 0)
    gmax = jnp.max(gscore, axis=0, keepdims=True)
    g_idx = jnp.min(jnp.where(gscore == gmax, gi, N_GROUPS), axis=0, keepdims=True)
    g_hot = gi == g_idx
    cand = [jnp.sum(jnp.where(g_hot, a[j], 0.0), axis=0, keepdims=True) for j in range(EPG)]
    scand = [jnp.sum(jnp.where(g_hot, sj[j], 0.0), axis=0, keepdims=True) for j in range(EPG)]

    def _first_argmax(vals):
        best, idx = vals[0], jnp.zeros((1, tm), jnp.int32)
        for j in range(1, EPG):
            better = vals[j] > best
            idx = jnp.where(better, j, idx)
            best = jnp.where(better, vals[j], best)
        return idx

    i1 = _first_argmax(cand)
    i2 = _first_argmax([jnp.where(i1 == j, -jnp.inf, cand[j]) for j in range(EPG)])

    def _pick(vals, idx):
        out = vals[0]
        for j in range(1, EPG):
            out = jnp.where(idx == j, vals[j], out)
        return out

    s1, s2 = _pick(scand, i1), _pick(scand, i2)
    den = s1 + s2
    gate1, gate2 = s1 / den, s2 / den
    first_is_a = i1 < i2
    ea, eb = jnp.minimum(i1, i2), jnp.maximum(i1, i2)
    g_a = jnp.where(first_is_a, gate1, gate2)
    g_b = jnp.where(first_is_a, gate2, gate1)
    xr = d // LANES
    _slab_rows(hx_ref, xr, tm, x_rows)[...] = jnp.broadcast_to(_to_rows(g_a, tm), (tm, LANES))
    _slab_rows(hx_ref, xr + 1, tm, x_rows)[...] = jnp.broadcast_to(_to_rows(g_b, tm), (tm, LANES))
    for k in range(xr + 2, x_rows):
        _slab_rows(hx_ref, k, tm, x_rows)[...] = jnp.zeros((tm, LANES), F32)
    pair =jnp.zeros((1, tm), jnp.int32)
    for q in range(N_PAIRS):
        pair = jnp.where((ea == PAIR_A[q]) & (eb == PAIR_B[q]), q, pair)
    cls = g_idx * N_PAIRS + pair
    cls_ref[...] = cls

    ki = lax.broadcasted_iota(jnp.int32, (CLASS_ROWS, tm), 0)
    c_hot = ki == cls
    onehot = jnp.where(c_hot, 1.0, 0.0)
    tr = lax.broadcasted_iota(jnp.int32, (tm, tm), 0)
    tc = lax.broadcasted_iota(jnp.int32, (tm, tm), 1)
    upper = jnp.where(tr < tc, 1.0, 0.0).astype(BF16)
    prefix = jnp.dot(onehot.astype(BF16), upper, preferred_element_type=F32)
    base = cnt_scr[:, 0:1]
    rank = jnp.sum(jnp.where(c_hot, prefix + base, 0.0), axis=0, keepdims=True)
    rank_ref[...] = rank.astype(jnp.int32)
    cnt_scr[...] = cnt_scr[...] + jnp.sum(onehot, axis=-1, keepdims=True)
    cnt_ref[...] = cnt_scr[...]

    pa_scr[0:HALO_A, :] = pa_scr[tm:tm + HALO_A, :]
    glu_scr[:, 0:HALO_C, :] = glu_scr[:, tm:tm + HALO_C, :]


def _const_spec(shape, single_buffer=False):
    nd = len(shape)
    kw = {"pipeline_mode": pl.Buffered(1)} if single_buffer else {}
    return pl.BlockSpec(shape, lambda i, _nd=nd: (0,) * _nd, **kw)


def _layer_spec(arr, layer, single_buffer=False):
    nd = arr.ndim - 1
    kw = {"pipeline_mode": pl.Buffered(1)} if single_buffer else {}
    return pl.BlockSpec((None,) + arr.shape[1:], lambda i, _nd=nd: (layer,) + (0,) * _nd, **kw)


def _mixer_call(x, p, layer, *, apply_ln_in, seq, alpha):
    t, d = x.shape
    tm = TM_MIX
    d_in = p["w_in"].shape[2]
    d_a, d_b, d_c = p["conv_a"].shape[2], p["ln_v_g"].shape[2], p["conv_c"].shape[2]
    heads, blk = p["gate_ws"].shape[1], p["gate_ws"].shape[2]
    assert t % tm == 0 and seq % tm == 0 and tm % blk == 0 and 3 * d_a + 2 * d_b + 2 * d_c == d_in
    assert d % LANES == 0 and p["conv_a"].shape[1] - 1 <= HALO_A and p["conv_c"].shape[1] - 1 <= HALO_C
    n_tiles = t // tm
    x_rows = d // LANES + SUBLANES
    assert x_rows % SUBLANES == 0 and d_c % LANES == 0
    kern = functools.partial(
        _mixer_kernel, apply_ln_in=apply_ln_in, tiles_per_seq=seq // tm, alpha=alpha,
        d_a=d_a, d_b=d_b, d_c=d_c, heads=heads, blk=blk, x_rows=x_rows)
    shared = [p["ln_in_g"], p["ln_in_b"]]
    per_layer = ["w_in", "conv_a", "gate_ws", "gate_bs_t", "ln_v_g", "ln_v_b", "conv_c", "conv_c_b",
                 "ln_c_g", "ln_c_b", "w_out", "ln1_g", "ln1_b"]
    router = [p["w_router"], p["b_router"]]
    in_specs = [pl.BlockSpec((tm, d), lambda i: (i, 0))]
    in_specs += [_const_spec(c.shape) for c in shared]
    in_specs += [_layer_spec(p[k], layer, single_buffer=k in ("w_in", "w_out")) for k in per_layer]
    in_specs += [_const_spec(c.shape) for c in router]
    tok_spec = pl.BlockSpec((1, tm), lambda i: (0, i))
    out_shape = (jax.ShapeDtypeStruct((t * x_rows, LANES), F32),
                 jax.ShapeDtypeStruct((1, t), jnp.int32),
                 jax.ShapeDtypeStruct((1, t), jnp.int32),
                 jax.ShapeDtypeStruct((CLASS_ROWS, LANES), F32))
    out_specs = (pl.BlockSpec((tm * x_rows, LANES), lambda i: (i, 0)),
                 tok_spec, tok_spec,
                 pl.BlockSpec((CLASS_ROWS, LANES), lambda i: (0, 0)))
    scratch = [pltpu.VMEM((tm + HALO_A, d_a), F32),
               pltpu.VMEM((d_c // LANES, tm + HALO_C, LANES), F32),
               pltpu.VMEM((tm, d_a), BF16),
               pltpu.VMEM((tm, d_b), BF16),
               pltpu.VMEM((tm, d_c), BF16),
               pltpu.VMEM((CLASS_ROWS, LANES), F32)]
    return pl.pallas_call(
        kern, grid=(n_tiles,), in_specs=in_specs, out_specs=out_specs, out_shape=out_shape,
        scratch_shapes=scratch, name="mixer",
        compiler_params=pltpu.CompilerParams(
            dimension_semantics=("arbitrary",), vmem_limit_bytes=VMEM_LIMIT),
    )(x, *shared, *[p[k] for k in per_layer], *router)


def _start_slab_gather(idx_ref, src_hbm, buf, sem, slot, n, rows_per, dst_stride=None):
    dst_stride = rows_per if dst_stride is None else dst_stride

    def body(c, carry):
        for u in range(DMA_UNROLL):
            r = c * DMA_UNROLL + u
            src_row = pl.multiple_of(idx_ref[0, 0, r] * rows_per, SUBLANES)
            dst_row = pl.multiple_of(r * dst_stride, SUBLANES)
            pltpu.make_async_copy(src_hbm.at[pl.ds(src_row, rows_per), :],
                                  buf.at[slot, pl.ds(dst_row, rows_per), :],
                                  sem.at[slot]).start()
        return carry
    lax.fori_loop(0, n // DMA_UNROLL, body, 0)


def _wait_slab_gather(src_hbm, buf, sem, slot, n, rows_per):
    pltpu.make_async_copy(src_hbm.at[pl.ds(0, n * rows_per), :],
                          buf.at[slot, pl.ds(0, n * rows_per), :], sem.at[slot]).wait()


def _experts_kernel(ba_ref, bb_ref, nv_ref, idx0_ref, idx1_ref, idx2_ref, hx_hbm,
                    w1a_ref, w3a_ref, w2a_ref, w1b_ref, w3b_ref, w2b_ref,
                    y_ref, xbuf, sem, *, rows_per):
    d, f = w1a_ref.shape
    xr = d // LANES
    bm = y_ref.shape[0] // xr
    i = pl.program_id(0)
    nv = nv_ref[0]
    slot = i % GATHER_SLOTS

    @pl.when(i == 0)
    def _():
        _start_slab_gather(idx0_ref, hx_hbm, xbuf, sem, 0, bm, rows_per)

    @pl.when((i == 0) & (nv > 1))
    def _():
        _start_slab_gather(idx1_ref, hx_hbm, xbuf, sem, 1, bm, rows_per)

    @pl.when(i + 2 < nv)
    def _():
        _start_slab_gather(idx2_ref, hx_hbm, xbuf, sem, (i + 2) % GATHER_SLOTS, bm, rows_per)

    @pl.when(i < nv)
    def _():
        _wait_slab_gather(hx_hbm, xbuf, sem, slot, bm, rows_per)
        xb = jnp.concatenate(
            [_slab_rows(xbuf, s, bm, rows_per, slot)[...].astype(BF16) for s in range(xr)], axis=1)
        reps = f // LANES

        def hidden(w1_ref, w3_ref, gate):
            h1 = jnp.dot(xb, w1_ref[...], preferred_element_type=F32)
            h3 = jnp.dot(xb, w3_ref[...], preferred_element_type=F32)
            return (h1 * _sigmoid(h1) * h3 * jnp.concatenate([gate] * reps, axis=1)).astype(BF16)

        h_a = hidden(w1a_ref, w3a_ref, _slab_rows(xbuf, xr, bm, rows_per, slot)[...])
        h_b = hidden(w1b_ref, w3b_ref, _slab_rows(xbuf, xr + 1, bm, rows_per, slot)[...])
        y = (jnp.dot(h_a, w2a_ref[...], preferred_element_type=F32)
             + jnp.dot(h_b, w2b_ref[...], preferred_element_type=F32))
        for s in range(xr):
            _slab_rows(y_ref, s, bm, xr)[...] = y[:, s * LANES:(s + 1) * LANES]

    @pl.when(i >= nv)
    def _():
        y_ref[...] = jnp.zeros(y_ref.shape, F32)


def _experts_call(hx, slot_tok, blk_a, blk_b, nvalid, w1, w3, w2, layer, d):
    bm = BM_MOE
    nb = slot_tok.shape[0]
    f = w1.shape[3]
    rows_per = d // LANES + SUBLANES
    y_rows = d // LANES
    assert f % LANES == 0

    def wspec(shape, which):
        if which == 0:
            return pl.BlockSpec(shape, lambda i, ba, bb, nv: (layer, ba[i], 0, 0))
        return pl.BlockSpec(shape, lambda i, ba, bb, nv: (layer, bb[i], 0, 0))

    def idx_spec(ahead):
        return pl.BlockSpec((1, 1, bm), lambda i, ba, bb, nv: (jnp.minimum(i + ahead, nb - 1), 0, 0),
                            memory_space=pltpu.SMEM)

    up, down = (None, None, d, f), (None, None, f, d)
    grid_spec = pltpu.PrefetchScalarGridSpec(
        num_scalar_prefetch=3,
        grid=(nb,),
        in_specs=[
            idx_spec(0), idx_spec(1), idx_spec(2),
            pl.BlockSpec(memory_space=pl.ANY),
            wspec(up, 0), wspec(up, 0), wspec(down, 0),
            wspec(up, 1), wspec(up, 1), wspec(down, 1),
        ],
        out_specs=pl.BlockSpec((bm * y_rows, LANES), lambda i, ba, bb, nv: (i, 0)),
        scratch_shapes=[pltpu.VMEM((GATHER_SLOTS, bm * rows_per, LANES), F32),
                        pltpu.SemaphoreType.DMA((GATHER_SLOTS,))],
    )
    return pl.pallas_call(
        functools.partial(_experts_kernel, rows_per=rows_per), grid_spec=grid_spec,
        out_shape=jax.ShapeDtypeStruct((nb * bm * y_rows, LANES), F32), name="experts",
        compiler_params=pltpu.CompilerParams(
            dimension_semantics=("arbitrary",), vmem_limit_bytes=VMEM_LIMIT),
    )(blk_a, blk_b, nvalid, slot_tok, slot_tok, slot_tok, hx, w1, w3, w2, w1, w3, w2)


def _combine_kernel(cur_idx_ref, nxt_idx_ref, y_hbm, hx_ref, g_ref, b_ref, o_ref, ybuf, sem,
                    *, alpha, x_rows):
    tm, d = o_ref.shape
    i = pl.program_id(0)
    n = pl.num_programs(0)
    slot = i % 2

    y_rows = d // LANES

    @pl.when(i == 0)
    def _():
        _start_slab_gather(cur_idx_ref, y_hbm, ybuf, sem, 0, tm, y_rows, x_rows)

    @pl.when(i + 1 < n)
    def _():
        _start_slab_gather(nxt_idx_ref, y_hbm, ybuf, sem, 1 - slot, tm, y_rows, x_rows)

    _wait_slab_gather(y_hbm, ybuf, sem, slot, tm, y_rows)
    r = jnp.concatenate(
        [alpha * _slab_rows(hx_ref, s, tm, x_rows)[...] + _slab_rows(ybuf, s, tm, x_rows, slot)[...]
         for s in range(y_rows)], axis=1)
    o_ref[...] = _layer_norm(r, g_ref[...], b_ref[...])


def _combine_call(y_sorted, dest, hx, g, b, layer, d, alpha):
    tm = TM_CMB
    rows_per = d // LANES + SUBLANES
    t = hx.shape[0] // rows_per
    n_tiles = t // tm
    ln_spec = pl.BlockSpec((None, 1, d), lambda i: (layer, 0, 0))
    return pl.pallas_call(
        functools.partial(_combine_kernel, alpha=alpha, x_rows=rows_per),
        grid=(n_tiles,),
        in_specs=[
            pl.BlockSpec((1, 1, tm), lambda i: (i, 0, 0), memory_space=pltpu.SMEM),
            pl.BlockSpec((1, 1, tm), lambda i: (jnp.minimum(i + 1, n_tiles - 1), 0, 0),
                         memory_space=pltpu.SMEM),
            pl.BlockSpec(memory_space=pl.ANY),
            pl.BlockSpec((tm * rows_per, LANES), lambda i: (i, 0)),
            ln_spec, ln_spec,
        ],
        out_specs=pl.BlockSpec((tm, d), lambda i: (i, 0)),
        out_shape=jax.ShapeDtypeStruct((t, d), F32),
        scratch_shapes=[pltpu.VMEM((2, tm * rows_per, LANES), F32), pltpu.SemaphoreType.DMA((2,))],
        name="combine",
        compiler_params=pltpu.CompilerParams(
            dimension_semantics=("arbitrary",), vmem_limit_bytes=VMEM_LIMIT),
    )(dest.reshape(n_tiles, 1, tm), dest.reshape(n_tiles, 1, tm), y_sorted, hx, g, b)


def _dispatch_plan(cls, rank, cnt, t):
    bm = BM_MOE
    nb = -(-t // bm) + N_CLASSES
    counts = cnt[:N_CLASSES, 0].astype(jnp.int32)
    nblk = (counts + bm - 1) // bm
    bend = jnp.cumsum(nblk)
    bstart = bend - nblk
    dest = bstart[cls] * bm + rank
    nvalid = bend[-1]
    blk = jnp.minimum(jnp.arange(nb, dtype=jnp.int32), nvalid - 1)
    blk_cls = jnp.sum((bend[None, :] <= blk[:, None]).astype(jnp.int32), axis=1)
    grp, pair = blk_cls // N_PAIRS, blk_cls % N_PAIRS
    blk_a = grp * EPG + jnp.asarray(PAIR_A, jnp.int32)[pair]
    blk_b = grp * EPG + jnp.asarray(PAIR_B, jnp.int32)[pair]
    slot_tok = jnp.zeros((nb * bm,), jnp.int32).at[dest].set(jnp.arange(t, dtype=jnp.int32))
    return dest, slot_tok.reshape(nb, 1, bm), blk_a, blk_b, nvalid.reshape(1)


def kernel(x, ln_in_g, ln_in_b, w_in, conv_a, gate_ws, gate_bs, ln_v_g, ln_v_b, conv_c, conv_c_b,
           ln_c_g, ln_c_b, w_out, ln1_g, ln1_b, w_router, b_router, w1, w3, w2, ln2_g, ln2_b):
    bsz, seq, d = x.shape
    depth = w_in.shape[0]
    t = bsz * seq
    alpha = float((2 * depth) ** 0.25)
    ne = N_GROUPS * EPG
    assert w_router.shape[1] == ne and w1.shape[1] == ne

    perm = np.array([g * EPG + j for j in range(EPG) for g in range(N_GROUPS)])
    wr_t = w_router.astype(F32).T[perm]
    wr_hi = wr_t.astype(BF16)
    wr_lo = (wr_t - wr_hi.astype(F32)).astype(BF16)

    rows = lambda v: v.reshape(v.shape[0], 1, -1).astype(F32)
    p = {
        "ln_in_g": ln_in_g.reshape(1, -1).astype(F32), "ln_in_b": ln_in_b.reshape(1, -1).astype(F32),
        "w_in": w_in.astype(BF16), "conv_a": conv_a, "gate_ws": gate_ws,
        "gate_bs_t": jnp.swapaxes(gate_bs, 1, 2), "ln_v_g": rows(ln_v_g), "ln_v_b": rows(ln_v_b),
        "conv_c": conv_c, "conv_c_b": rows(conv_c_b), "ln_c_g": rows(ln_c_g), "ln_c_b": rows(ln_c_b),
        "w_out": w_out.astype(BF16), "ln1_g": rows(ln1_g), "ln1_b": rows(ln1_b),
        "w_router": jnp.concatenate([wr_hi, wr_lo], axis=0),
        "b_router": b_router.astype(F32)[perm][:, None],
    }
    w1b, w3b, w2b = w1.astype(BF16), w3.astype(BF16), w2.astype(BF16)
    ln2g, ln2b = rows(ln2_g), rows(ln2_b)

    h = x.reshape(t, d)
    for l in range(depth):
        hx, cls, rank, cnt = _mixer_call(h, p, l, apply_ln_in=(l == 0), seq=seq, alpha=alpha)
        dest, slot_tok, blk_a, blk_b, nvalid = _dispatch_plan(cls[0], rank[0], cnt, t)
        y_sorted = _experts_call(hx, slot_tok, blk_a, blk_b, nvalid, w1b, w3b, w2b, l, d)
        h = _combine_call(y_sorted, dest, hx, ln2g, ln2b, l, d, alpha)
    return h.reshape(bsz, seq, d)
```

```python
import functools

import numpy as np
import jax
import jax.numpy as jnp
from jax import lax
from jax.experimental import pallas as pl
from jax.experimental.pallas import tpu as pltpu

F32 = jnp.float32
BF16 = jnp.bfloat16

LN_EPS = 1e-5
CHUNK = 64
N_GROUPS = 8
EPG = 4
PAIR_A = (0, 0, 0, 1, 1, 2)
PAIR_B = (1, 2, 3, 3, 2, 3)
N_PAIRS = len(PAIR_A)
N_CLASSES = N_GROUPS * N_PAIRS
CLASS_ROWS = 64
LANES = 128
SUBLANES = 8
HALO_A = 8
HALO_C = 32
VMEM_LIMIT = 60 * 1024 * 1024
DMA_UNROLL = 8
GATHER_SLOTS = 3

TM_MIX = 256
BM_MOE = 256
TM_CMB = 512

_NT = (((1,), (1,)), ((), ()))


def _layer_norm(x, g, b):
    mu = jnp.mean(x, axis=-1, keepdims=True)
    xc = x - mu
    var = jnp.mean(xc * xc, axis=-1, keepdims=True)
    return xc * lax.rsqrt(var + LN_EPS) * g + b


def _sigmoid(x):
    return 1.0 / (1.0 + jnp.exp(-x))


def _to_rows(v, n):
    r = lax.broadcasted_iota(jnp.int32, (n, n), 0)
    c = lax.broadcasted_iota(jnp.int32, (n, n), 1)
    return jnp.sum(jnp.where(r == c, jnp.broadcast_to(v, (n, n)), 0.0), axis=-1, keepdims=True)


def _slab_rows(ref, k, n, rows_per, slot=None):
    win = pl.ds(k, n, stride=rows_per)
    return ref.at[win, :] if slot is None else ref.at[slot, win, :]


def _mixer_kernel(x_ref, lng_ref, lnb_ref, win_ref, conva_ref, ws_ref, gbt_ref, lnvg_ref, lnvb_ref,
                  convc_ref, convcb_ref, lncg_ref, lncb_ref, wout_ref, ln1g_ref, ln1b_ref,
                  wr_ref, br_ref, w1f_ref, w3f_ref, w2f_ref,
                  hx_ref, cls_ref, rank_ref, cnt_ref, w1b_ref, w3b_ref, w2b_ref,
                  pa_scr, glu_scr, ya_scr, yb_scr, yc_scr, cnt_scr,
                  *, apply_ln_in, tiles_per_seq, alpha, d_a, d_b, d_c, heads, blk, x_rows):
    tm, d = x_ref.shape
    hd = d_b // heads
    i = pl.program_id(0)

    w1b_ref[...] = w1f_ref[...].astype(BF16)
    w3b_ref[...] = w3f_ref[...].astype(BF16)
    w2b_ref[...] = w2f_ref[...].astype(BF16)

    @pl.when(i % tiles_per_seq == 0)
    def _():
        pa_scr[0:HALO_A, :] = jnp.zeros((HALO_A, d_a), F32)
        glu_scr[:, 0:HALO_C, :] = jnp.zeros((d_c // LANES, HALO_C, LANES), F32)

    @pl.when(i == 0)
    def _():
        cnt_scr[...] = jnp.zeros(cnt_scr.shape, F32)

    x = x_ref[...]
    if apply_ln_in:
        x = _layer_norm(x, lng_ref[...], lnb_ref[...])
    xb = x.astype(BF16)

    c0 = 3 * d_a + 2 * d_b
    h_c = jnp.dot(xb, win_ref[:, c0:c0 + 2 * d_c], preferred_element_type=F32)
    glu = h_c[:, 0:d_c] * _sigmoid(h_c[:, d_c:2 * d_c])
    kc = convc_ref.shape[0]
    parts = []
    for j in range(d_c // LANES):
        ln = slice(j * LANES, (j + 1) * LANES)
        glu_scr[j, HALO_C:HALO_C + tm, :] = glu[:, ln]
        acc_j = convcb_ref[:, ln] + convc_ref[kc - 1:kc, ln] * glu[:, ln]
        for k in range(kc - 1):
            off = HALO_C - (kc - 1) + k
            acc_j = acc_j + convc_ref[k:k + 1, ln] * glu_scr[j, off:off + tm, :]
        parts.append(acc_j)
    acc = jnp.concatenate(parts, axis=1)
    yc = _layer_norm(acc, lncg_ref[...], lncb_ref[...])
    yc_scr[...] = (yc * _sigmoid(yc)).astype(BF16)

    h_a = jnp.dot(xb, win_ref[:, 0:3 * d_a], preferred_element_type=F32)
    p = h_a[:, d_a:2 * d_a] * h_a[:, 2 * d_a:3 * d_a]
    pa_scr[HALO_A:HALO_A + tm, :] = p
    ka = conva_ref.shape[0]
    conv = conva_ref[ka - 1:ka, :] * p
    for k in range(ka - 1):
        off = HALO_A - (ka - 1) + k
        conv = conv + conva_ref[k:k + 1, :] * pa_scr[off:off + tm, :]
    ya_scr[...] = (h_a[:, 0:d_a] * conv).astype(BF16)

    c0 = 3 * d_a
    h_b = jnp.dot(xb, win_ref[:, c0:c0 + 2 * d_b], preferred_element_type=F32)
    u = h_b[:, 0:d_b]
    vn = _layer_norm(h_b[:, d_b:2 * d_b], lnvg_ref[...], lnvb_ref[...]).astype(BF16)
    ri = lax.broadcasted_iota(jnp.int32, (blk, blk), 0) // CHUNK
    ci = lax.broadcasted_iota(jnp.int32, (blk, blk), 1) // CHUNK
    for h in range(heads):
        ws_h = jnp.where(ci <= ri, ws_ref[h], 0.0).astype(BF16)
        bias_h = gbt_ref[:, h:h + 1]
        for n in range(tm // blk):
            br = slice(n * blk, (n + 1) * blk)
            cols = slice(h * hd, (h + 1) * hd)
            z = jnp.dot(ws_h, vn[br, cols], preferred_element_type=F32) + bias_h
            yb_scr[br, cols] = (u[br, cols] * z).astype(BF16)

    mix = (jnp.dot(yc_scr[...], wout_ref[d_a + d_b:d_a + d_b + d_c, :], preferred_element_type=F32)
           + jnp.dot(ya_scr[...], wout_ref[0:d_a, :], preferred_element_type=F32)
           + jnp.dot(yb_scr[...], wout_ref[d_a:d_a + d_b, :], preferred_element_type=F32))
    h1 = _layer_norm(alpha * x + mix, ln1g_ref[...], ln1b_ref[...])
    for s in range(d // LANES):
        _slab_rows(hx_ref, s, tm, x_rows)[...] = h1[:, s * LANES:(s + 1) * LANES]

    ne = N_GROUPS * EPG
    hi = h1.astype(BF16)
    lo_part = (h1 - hi.astype(F32)).astype(BF16)
    o1 = lax.dot_general(wr_ref[...], hi, _NT, preferred_element_type=F32)
    o2 = lax.dot_general(wr_ref[0:ne, :], lo_part, _NT, preferred_element_type=F32)
    s = _sigmoid(o1[0:ne, :] + o1[ne:2 * ne, :] + o2)
    sel = s + br_ref[...]
    a = [sel[j * N_GROUPS:(j + 1) * N_GROUPS, :] for j in range(EPG)]
    sj = [s[j * N_GROUPS:(j + 1) * N_GROUPS, :] for j in range(EPG)]
    m01, n01 = jnp.maximum(a[0], a[1]), jnp.minimum(a[0], a[1])
    m23, n23 = jnp.maximum(a[2], a[3]), jnp.minimum(a[2], a[3])
    gscore = jnp.maximum(m01, m23) + jnp.maximum(jnp.minimum(m01, m23), jnp.maximum(n01, n23))
    gi = lax.broadcasted_iota(jnp.int32, (N_GROUPS, tm), 0)
    gmax = jnp.max(gscore, axis=0, keepdims=True)
    g_idx = jnp.min(jnp.where(gscore == gmax, gi, N_GROUPS), axis=0, keepdims=True)
    g_hot = gi == g_idx
    cand = [jnp.sum(jnp.where(g_hot, a[j], 0.0), axis=0, keepdims=True) for j in range(EPG)]
    scand = [jnp.sum(jnp.where(g_hot, sj[j], 0.0), axis=0, keepdims=True) for j in range(EPG)]

    def _first_argmax(vals):
        best, idx = vals[0], jnp.zeros((1, tm), jnp.int32)
        for j in range(1, EPG):
            better = vals[j] > best
            idx = jnp.where(better, j, idx)
            best = jnp.where(better, vals[j], best)
        return idx

    i1 = _first_argmax(cand)
    i2 = _first_argmax([jnp.where(i1 == j, -jnp.inf, cand[j]) for j in range(EPG)])

    def _pick(vals, idx):
        out = vals[0]
        for j in range(1, EPG):
            out = jnp.where(idx == j, vals[j], out)
        return out

    s1, s2 = _pick(scand, i1), _pick(scand, i2)
    den = s1 + s2
    gate1, gate2 = s1 / den, s2 / den
    first_is_a = i1 < i2
    ea, eb = jnp.minimum(i1, i2), jnp.maximum(i1, i2)
    g_a = jnp.where(first_is_a, gate1, gate2)
    g_b = jnp.where(first_is_a, gate2, gate1)
    xr = d // LANES
    _slab_rows(hx_ref, xr, tm, x_rows)[...] = jnp.broadcast_to(_to_rows(g_a, tm), (tm, LANES))
    _slab_rows(hx_ref, xr + 1, tm, x_rows)[...] = jnp.broadcast_to(_to_rows(g_b, tm), (tm, LANES))
    for k in range(xr + 2, x_rows):
        _slab_rows(hx_ref, k, tm, x_rows)[...] = jnp.zeros((tm, LANES), F32)
    pair =jnp.zeros((1, tm), jnp.int32)
    for q in range(N_PAIRS):
        pair = jnp.where((ea == PAIR_A[q]) & (eb == PAIR_B[q]), q, pair)
    cls = g_idx * N_PAIRS + pair
    cls_ref[...] = cls

    ki = lax.broadcasted_iota(jnp.int32, (CLASS_ROWS, tm), 0)
    c_hot = ki == cls
    onehot = jnp.where(c_hot, 1.0, 0.0)
    tr = lax.broadcasted_iota(jnp.int32, (tm, tm), 0)
    tc = lax.broadcasted_iota(jnp.int32, (tm, tm), 1)
    upper = jnp.where(tr < tc, 1.0, 0.0).astype(BF16)
    prefix = jnp.dot(onehot.astype(BF16), upper, preferred_element_type=F32)
    base = cnt_scr[:, 0:1]
    rank = jnp.sum(jnp.where(c_hot, prefix + base, 0.0), axis=0, keepdims=True)
    rank_ref[...] = rank.astype(jnp.int32)
    cnt_scr[...] = cnt_scr[...] + jnp.sum(onehot, axis=-1, keepdims=True)
    cnt_ref[...] = cnt_scr[...]

    pa_scr[0:HALO_A, :] = pa_scr[tm:tm + HALO_A, :]
    glu_scr[:, 0:HALO_C, :] = glu_scr[:, tm:tm + HALO_C, :]


def _const_spec(shape, single_buffer=False):
    nd = len(shape)
    kw = {"pipeline_mode": pl.Buffered(1)} if single_buffer else {}
    return pl.BlockSpec(shape, lambda i, _nd=nd: (0,) * _nd, **kw)


def _layer_spec(arr, layer, single_buffer=False):
    nd = arr.ndim - 1
    kw = {"pipeline_mode": pl.Buffered(1)} if single_buffer else {}
    return pl.BlockSpec((None,) + arr.shape[1:], lambda i, _nd=nd: (layer,) + (0,) * _nd, **kw)


def _mixer_call(x, p, expert_w, layer, *, apply_ln_in, seq, alpha):
    t, d = x.shape
    tm = TM_MIX
    d_in = p["w_in"].shape[2]
    d_a, d_b, d_c = p["conv_a"].shape[2], p["ln_v_g"].shape[2], p["conv_c"].shape[2]
    heads, blk = p["gate_ws"].shape[1], p["gate_ws"].shape[2]
    assert t % tm == 0 and seq % tm == 0 and tm % blk == 0 and 3 * d_a + 2 * d_b + 2 * d_c == d_in
    assert d % LANES == 0 and p["conv_a"].shape[1] - 1 <= HALO_A and p["conv_c"].shape[1] - 1 <= HALO_C
    n_tiles = t // tm
    x_rows = d // LANES + SUBLANES
    assert x_rows % SUBLANES == 0 and d_c % LANES == 0
    kern = functools.partial(
        _mixer_kernel, apply_ln_in=apply_ln_in, tiles_per_seq=seq // tm, alpha=alpha,
        d_a=d_a, d_b=d_b, d_c=d_c, heads=heads, blk=blk, x_rows=x_rows)
    shared = [p["ln_in_g"], p["ln_in_b"]]
    per_layer = ["w_in", "conv_a", "gate_ws", "gate_bs_t", "ln_v_g", "ln_v_b", "conv_c", "conv_c_b",
                 "ln_c_g", "ln_c_b", "w_out", "ln1_g", "ln1_b"]
    router = [p["w_router"], p["b_router"]]
    in_specs = [pl.BlockSpec((tm, d), lambda i: (i, 0))]
    in_specs += [_const_spec(c.shape) for c in shared]
    in_specs += [_layer_spec(p[k], layer, single_buffer=k in ("w_in", "w_out")) for k in per_layer]
    in_specs += [_const_spec(c.shape) for c in router]
    experts = [w.reshape(w.shape[0], -1, w.shape[-1]) for w in expert_w]
    assert all(w.shape[1] % (n_tiles * 2 * SUBLANES) == 0 for w in experts)
    in_specs += [pl.BlockSpec((None, w.shape[1] // n_tiles, w.shape[2]), lambda i: (layer, i, 0))
                 for w in experts]
    tok_spec = pl.BlockSpec((1, tm), lambda i: (0, i))
    out_shape = (jax.ShapeDtypeStruct((t * x_rows, LANES), F32),
                 jax.ShapeDtypeStruct((1, t), jnp.int32),
                 jax.ShapeDtypeStruct((1, t), jnp.int32),
                 jax.ShapeDtypeStruct((CLASS_ROWS, LANES), F32))
    out_shape += tuple(jax.ShapeDtypeStruct(w.shape[1:], BF16) for w in experts)
    out_specs = (pl.BlockSpec((tm * x_rows, LANES), lambda i: (i, 0)),
                 tok_spec, tok_spec,
                 pl.BlockSpec((CLASS_ROWS, LANES), lambda i: (0, 0)))
    out_specs += tuple(pl.BlockSpec((w.shape[1] // n_tiles, w.shape[2]), lambda i: (i, 0))
                       for w in experts)
    scratch = [pltpu.VMEM((tm + HALO_A, d_a), F32),
               pltpu.VMEM((d_c // LANES, tm + HALO_C, LANES), F32),
               pltpu.VMEM((tm, d_a), BF16),
               pltpu.VMEM((tm, d_b), BF16),
               pltpu.VMEM((tm, d_c), BF16),
               pltpu.VMEM((CLASS_ROWS, LANES), F32)]
    return pl.pallas_call(
        kern, grid=(n_tiles,), in_specs=in_specs, out_specs=out_specs, out_shape=out_shape,
        scratch_shapes=scratch, name="mixer",
        compiler_params=pltpu.CompilerParams(
            dimension_semantics=("arbitrary",), vmem_limit_bytes=VMEM_LIMIT),
    )(x, *shared, *[p[k] for k in per_layer], *router, *experts)


def _start_slab_gather(idx_ref, src_hbm, buf, sem, slot, n, rows_per, dst_stride=None):
    dst_stride = rows_per if dst_stride is None else dst_stride

    def body(c, carry):
        for u in range(DMA_UNROLL):
            r = c * DMA_UNROLL + u
            src_row = pl.multiple_of(idx_ref[0, 0, r] * rows_per, SUBLANES)
            dst_row = pl.multiple_of(r * dst_stride, SUBLANES)
            pltpu.make_async_copy(src_hbm.at[pl.ds(src_row, rows_per), :],
                                  buf.at[slot, pl.ds(dst_row, rows_per), :],
                                  sem.at[slot]).start()
        return carry
    lax.fori_loop(0, n // DMA_UNROLL, body, 0)


def _wait_slab_gather(src_hbm, buf, sem, slot, n, rows_per):
    pltpu.make_async_copy(src_hbm.at[pl.ds(0, n * rows_per), :],
                          buf.at[slot, pl.ds(0, n * rows_per), :], sem.at[slot]).wait()


def _experts_kernel(ba_ref, bb_ref, nv_ref, idx0_ref, idx1_ref, idx2_ref, hx_hbm,
                    w1a_ref, w3a_ref, w2a_ref, w1b_ref, w3b_ref, w2b_ref,
                    y_ref, xbuf, sem, *, rows_per):
    d, f = w1a_ref.shape
    xr = d // LANES
    bm = y_ref.shape[0] // xr
    i = pl.program_id(0)
    nv = nv_ref[0]
    slot = i % GATHER_SLOTS

    @pl.when(i == 0)
    def _():
        _start_slab_gather(idx0_ref, hx_hbm, xbuf, sem, 0, bm, rows_per)

    @pl.when((i == 0) & (nv > 1))
    def _():
        _start_slab_gather(idx1_ref, hx_hbm, xbuf, sem, 1, bm, rows_per)

    @pl.when(i + 2 < nv)
    def _():
        _start_slab_gather(idx2_ref, hx_hbm, xbuf, sem, (i + 2) % GATHER_SLOTS, bm, rows_per)

    @pl.when(i < nv)
    def _():
        _wait_slab_gather(hx_hbm, xbuf, sem, slot, bm, rows_per)
        xb = jnp.concatenate(
            [_slab_rows(xbuf, s, bm, rows_per, slot)[...].astype(BF16) for s in range(xr)], axis=1)
        reps = f // LANES

        def hidden(w1_ref, w3_ref, gate):
            h1 = jnp.dot(xb, w1_ref[...], preferred_element_type=F32)
            h3 = jnp.dot(xb, w3_ref[...], preferred_element_type=F32)
            return (h1 * _sigmoid(h1) * h3 * jnp.concatenate([gate] * reps, axis=1)).astype(BF16)

        h_a = hidden(w1a_ref, w3a_ref, _slab_rows(xbuf, xr, bm, rows_per, slot)[...])
        h_b = hidden(w1b_ref, w3b_ref, _slab_rows(xbuf, xr + 1, bm, rows_per, slot)[...])
        y = (jnp.dot(h_a, w2a_ref[...], preferred_element_type=F32)
             + jnp.dot(h_b, w2b_ref[...], preferred_element_type=F32))
        for s in range(xr):
            _slab_rows(y_ref, s, bm, xr)[...] = y[:, s * LANES:(s + 1) * LANES]

    @pl.when(i >= nv)
    def _():
        y_ref[...] = jnp.zeros(y_ref.shape, F32)


def _experts_call(hx, slot_tok, blk_a, blk_b, nvalid, w1, w3, w2, d):
    bm = BM_MOE
    nb = slot_tok.shape[0]
    f = w1.shape[2]
    rows_per = d // LANES + SUBLANES
    y_rows = d // LANES
    assert f % LANES == 0

    def wspec(shape, which):
        if which == 0:
            return pl.BlockSpec(shape, lambda i, ba, bb, nv: (ba[i], 0, 0))
        return pl.BlockSpec(shape, lambda i, ba, bb, nv: (bb[i], 0, 0))

    def idx_spec(ahead):
        return pl.BlockSpec((1, 1, bm), lambda i, ba, bb, nv: (jnp.minimum(i + ahead, nb - 1), 0, 0),
                            memory_space=pltpu.SMEM)

    up, down = (None, d, f), (None, f, d)
    grid_spec = pltpu.PrefetchScalarGridSpec(
        num_scalar_prefetch=3,
        grid=(nb,),
        in_specs=[
            idx_spec(0), idx_spec(1), idx_spec(2),
            pl.BlockSpec(memory_space=pl.ANY),
            wspec(up, 0), wspec(up, 0), wspec(down, 0),
            wspec(up, 1), wspec(up, 1), wspec(down, 1),
        ],
        out_specs=pl.BlockSpec((bm * y_rows, LANES), lambda i, ba, bb, nv: (i, 0)),
        scratch_shapes=[pltpu.VMEM((GATHER_SLOTS, bm * rows_per, LANES), F32),
                        pltpu.SemaphoreType.DMA((GATHER_SLOTS,))],
    )
    return pl.pallas_call(
        functools.partial(_experts_kernel, rows_per=rows_per), grid_spec=grid_spec,
        out_shape=jax.ShapeDtypeStruct((nb * bm * y_rows, LANES), F32), name="experts",
        compiler_params=pltpu.CompilerParams(
            dimension_semantics=("arbitrary",), vmem_limit_bytes=VMEM_LIMIT),
    )(blk_a, blk_b, nvalid, slot_tok, slot_tok, slot_tok, hx, w1, w3, w2, w1, w3, w2)


def _combine_kernel(cur_idx_ref, nxt_idx_ref, y_hbm, hx_ref, g_ref, b_ref, o_ref, ybuf, sem,
                    *, alpha, x_rows):
    tm, d = o_ref.shape
    i = pl.program_id(0)
    n = pl.num_programs(0)
    slot = i % 2

    y_rows = d // LANES

    @pl.when(i == 0)
    def _():
        _start_slab_gather(cur_idx_ref, y_hbm, ybuf, sem, 0, tm, y_rows, x_rows)

    @pl.when(i + 1 < n)
    def _():
        _start_slab_gather(nxt_idx_ref, y_hbm, ybuf, sem, 1 - slot, tm, y_rows, x_rows)

    _wait_slab_gather(y_hbm, ybuf, sem, slot, tm, y_rows)
    r = jnp.concatenate(
        [alpha * _slab_rows(hx_ref, s, tm, x_rows)[...] + _slab_rows(ybuf, s, tm, x_rows, slot)[...]
         for s in range(y_rows)], axis=1)
    o_ref[...] = _layer_norm(r, g_ref[...], b_ref[...])


def _combine_call(y_sorted, dest, hx, g, b, layer, d, alpha):
    tm = TM_CMB
    rows_per = d // LANES + SUBLANES
    t = hx.shape[0] // rows_per
    n_tiles = t // tm
    ln_spec = pl.BlockSpec((None, 1, d), lambda i: (layer, 0, 0))
    return pl.pallas_call(
        functools.partial(_combine_kernel, alpha=alpha, x_rows=rows_per),
        grid=(n_tiles,),
        in_specs=[
            pl.BlockSpec((1, 1, tm), lambda i: (i, 0, 0), memory_space=pltpu.SMEM),
            pl.BlockSpec((1, 1, tm), lambda i: (jnp.minimum(i + 1, n_tiles - 1), 0, 0),
                         memory_space=pltpu.SMEM),
            pl.BlockSpec(memory_space=pl.ANY),
            pl.BlockSpec((tm * rows_per, LANES), lambda i: (i, 0)),
            ln_spec, ln_spec,
        ],
        out_specs=pl.BlockSpec((tm, d), lambda i: (i, 0)),
        out_shape=jax.ShapeDtypeStruct((t, d), F32),
        scratch_shapes=[pltpu.VMEM((2, tm * rows_per, LANES), F32), pltpu.SemaphoreType.DMA((2,))],
        name="combine",
        compiler_params=pltpu.CompilerParams(
            dimension_semantics=("arbitrary",), vmem_limit_bytes=VMEM_LIMIT),
    )(dest.reshape(n_tiles, 1, tm), dest.reshape(n_tiles, 1, tm), y_sorted, hx, g, b)


def _dispatch_plan(cls, rank, cnt, t):
    bm = BM_MOE
    nb = -(-t // bm) + N_CLASSES
    counts = cnt[:N_CLASSES, 0].astype(jnp.int32)
    nblk = (counts + bm - 1) // bm
    bend = jnp.cumsum(nblk)
    bstart = bend - nblk
    dest = bstart[cls] * bm + rank
    nvalid = bend[-1]
    blk = jnp.minimum(jnp.arange(nb, dtype=jnp.int32), nvalid - 1)
    blk_cls = jnp.sum((bend[None, :] <= blk[:, None]).astype(jnp.int32), axis=1)
    grp, pair = blk_cls // N_PAIRS, blk_cls % N_PAIRS
    blk_a = grp * EPG + jnp.asarray(PAIR_A, jnp.int32)[pair]
    blk_b = grp * EPG + jnp.asarray(PAIR_B, jnp.int32)[pair]
    slot_tok = jnp.zeros((nb * bm,), jnp.int32).at[dest].set(jnp.arange(t, dtype=jnp.int32))
    return dest, slot_tok.reshape(nb, 1, bm), blk_a, blk_b, nvalid.reshape(1)


def kernel(x, ln_in_g, ln_in_b, w_in, conv_a, gate_ws, gate_bs, ln_v_g, ln_v_b, conv_c, conv_c_b,
           ln_c_g, ln_c_b, w_out, ln1_g, ln1_b, w_router, b_router, w1, w3, w2, ln2_g, ln2_b):
    bsz, seq, d = x.shape
    depth = w_in.shape[0]
    t = bsz * seq
    alpha = float((2 * depth) ** 0.25)
    ne = N_GROUPS * EPG
    assert w_router.shape[1] == ne and w1.shape[1] == ne

    perm = np.array([g * EPG + j for j in range(EPG) for g in range(N_GROUPS)])
    wr_t = w_router.astype(F32).T[perm]
    wr_hi = wr_t.astype(BF16)
    wr_lo = (wr_t - wr_hi.astype(F32)).astype(BF16)

    rows = lambda v: v.reshape(v.shape[0], 1, -1).astype(F32)
    p = {
        "ln_in_g": ln_in_g.reshape(1, -1).astype(F32), "ln_in_b": ln_in_b.reshape(1, -1).astype(F32),
        "w_in": w_in.astype(BF16), "conv_a": conv_a, "gate_ws": gate_ws,
        "gate_bs_t": jnp.swapaxes(gate_bs, 1, 2), "ln_v_g": rows(ln_v_g), "ln_v_b": rows(ln_v_b),
        "conv_c": conv_c, "conv_c_b": rows(conv_c_b), "ln_c_g": rows(ln_c_g), "ln_c_b": rows(ln_c_b),
        "w_out": w_out.astype(BF16), "ln1_g": rows(ln1_g), "ln1_b": rows(ln1_b),
        "w_router": jnp.concatenate([wr_hi, wr_lo], axis=0),
        "b_router": b_router.astype(F32)[perm][:, None],
    }
    ln2g, ln2b = rows(ln2_g), rows(ln2_b)

    h = x.reshape(t, d)
    for l in range(depth):
        hx, cls, rank, cnt, w1b, w3b, w2b = _mixer_call(
            h, p, (w1, w3, w2), l, apply_ln_in=(l == 0), seq=seq, alpha=alpha)
        dest, slot_tok, blk_a, blk_b, nvalid = _dispatch_plan(cls[0], rank[0], cnt, t)
        y_sorted = _experts_call(hx, slot_tok, blk_a, blk_b, nvalid, w1b.reshape(w1.shape[1:]),
                                 w3b.reshape(w3.shape[1:]), w2b.reshape(w2.shape[1:]), d)
        h = _combine_call(y_sorted, dest, hx, ln2g, ln2b, l, d, alpha)
    return h.reshape(bsz, seq, d)
```

```python
import functools

import numpy as np
import jax
import jax.numpy as jnp
from jax import lax
from jax.experimental import pallas as pl
from jax.experimental.pallas import tpu as pltpu

F32 = jnp.float32
BF16 = jnp.bfloat16

LN_EPS = 1e-5
CHUNK = 64
N_GROUPS = 8
EPG = 4
PAIR_A = (0, 0, 0, 1, 1, 2)
PAIR_B = (1, 2, 3, 3, 2, 3)
N_PAIRS = len(PAIR_A)
N_CLASSES = N_GROUPS * N_PAIRS
CLASS_ROWS = 64
LANES = 128
SUBLANES = 8
HALO_A = 8
HALO_C = 32
VMEM_LIMIT = 60 * 1024 * 1024
DMA_UNROLL = 8
GATHER_SLOTS = 3

TM_MIX = 256
BM_MOE = 256
TM_CMB = 512


def _layer_norm(x, g, b):
    mu = jnp.mean(x, axis=-1, keepdims=True)
    xc = x - mu
    var = jnp.mean(xc * xc, axis=-1, keepdims=True)
    return xc * lax.rsqrt(var + LN_EPS) * g + b


def _sigmoid(x):
    return 1.0 / (1.0 + jnp.exp(-x))


def _to_rows(v, n):
    r = lax.broadcasted_iota(jnp.int32, (n, n), 0)
    c = lax.broadcasted_iota(jnp.int32, (n, n), 1)
    return jnp.sum(jnp.where(r == c, jnp.broadcast_to(v, (n, n)), 0.0), axis=-1, keepdims=True)


def _slab_rows(ref, k, n, rows_per, slot=None):
    win = pl.ds(k, n, stride=rows_per)
    return ref.at[win, :] if slot is None else ref.at[slot, win, :]


def _mixer_kernel(x_ref, lng_ref, lnb_ref, win_ref, conva_ref, ws_ref, gbt_ref, lnvg_ref, lnvb_ref,
                  convc_ref, convcb_ref, lncg_ref, lncb_ref, wout_ref, ln1g_ref, ln1b_ref,
                  wr_ref, br_ref, w1f_ref, w3f_ref, w2f_ref,
                  hx_ref, cls_ref, rank_ref, cnt_ref, w1b_ref, w3b_ref, w2b_ref,
                  pa_scr, glu_scr, y_scr, cnt_scr,
                  *, apply_ln_in, tiles_per_seq, alpha, d_a, d_b, d_c, heads, blk, x_rows):
    tm, d = x_ref.shape
    hd = d_b // heads
    i = pl.program_id(0)

    w1b_ref[...] = w1f_ref[...].astype(BF16)
    w3b_ref[...] = w3f_ref[...].astype(BF16)
    w2b_ref[...] = w2f_ref[...].astype(BF16)

    @pl.when(i % tiles_per_seq == 0)
    def _():
        pa_scr[0:HALO_A, :] = jnp.zeros((HALO_A, d_a), F32)
        glu_scr[:, 0:HALO_C, :] = jnp.zeros((d_c // LANES, HALO_C, LANES), F32)

    @pl.when(i == 0)
    def _():
        cnt_scr[...] = jnp.zeros(cnt_scr.shape, F32)

    x = x_ref[...]
    if apply_ln_in:
        x = _layer_norm(x, lng_ref[...], lnb_ref[...])
    xb = x.astype(BF16)

    c0 = 3 * d_a + 2 * d_b
    h_c = jnp.dot(xb, win_ref[:, c0:c0 + 2 * d_c], preferred_element_type=F32)
    glu = h_c[:, 0:d_c] * _sigmoid(h_c[:, d_c:2 * d_c])
    kc = convc_ref.shape[0]
    parts = []
    for j in range(d_c // LANES):
        ln = slice(j * LANES, (j + 1) * LANES)
        glu_scr[j, HALO_C:HALO_C + tm, :] = glu[:, ln]
        acc_j = convcb_ref[:, ln] + convc_ref[kc - 1:kc, ln] * glu[:, ln]
        for k in range(kc - 1):
            off = HALO_C - (kc - 1) + k
            acc_j = acc_j + convc_ref[k:k + 1, ln] * glu_scr[j, off:off + tm, :]
        parts.append(acc_j)
    acc = jnp.concatenate(parts, axis=1)
    yc = _layer_norm(acc, lncg_ref[...], lncb_ref[...])
    y_scr[:, d_a + d_b:d_a + d_b + d_c] = (yc * _sigmoid(yc)).astype(BF16)

    h_a = jnp.dot(xb, win_ref[:, 0:3 * d_a], preferred_element_type=F32)
    p = h_a[:, d_a:2 * d_a] * h_a[:, 2 * d_a:3 * d_a]
    pa_scr[HALO_A:HALO_A + tm, :] = p
    ka = conva_ref.shape[0]
    conv = conva_ref[ka - 1:ka, :] * p
    for k in range(ka - 1):
        off = HALO_A - (ka - 1) + k
        conv = conv + conva_ref[k:k + 1, :] * pa_scr[off:off + tm, :]
    y_scr[:, 0:d_a] = (h_a[:, 0:d_a] * conv).astype(BF16)

    c0 = 3 * d_a
    h_b = jnp.dot(xb, win_ref[:, c0:c0 + 2 * d_b], preferred_element_type=F32)
    u = h_b[:, 0:d_b]
    vn = _layer_norm(h_b[:, d_b:2 * d_b], lnvg_ref[...], lnvb_ref[...]).astype(BF16)
    ri = lax.broadcasted_iota(jnp.int32, (blk, blk), 0) // CHUNK
    ci = lax.broadcasted_iota(jnp.int32, (blk, blk), 1) // CHUNK
    for h in range(heads):
        ws_h = jnp.where(ci <= ri, ws_ref[h], 0.0).astype(BF16)
        bias_h = gbt_ref[:, h:h + 1]
        for n in range(tm // blk):
            br = slice(n * blk, (n + 1) * blk)
            cols = slice(h * hd, (h + 1) * hd)
            z = jnp.dot(ws_h, vn[br, cols], preferred_element_type=F32) + bias_h
            y_scr[br, d_a + h * hd:d_a + (h + 1) * hd] = (u[br, cols] * z).astype(BF16)

    mix = jnp.dot(y_scr[...], wout_ref[...], preferred_element_type=F32)
    h1 = _layer_norm(alpha * x + mix, ln1g_ref[...], ln1b_ref[...])
    for s in range(d // LANES):
        _slab_rows(hx_ref, s, tm, x_rows)[...] = h1[:, s * LANES:(s + 1) * LANES]

    ne = N_GROUPS * EPG
    hi = h1.astype(BF16)
    lo_part = (h1 - hi.astype(F32)).astype(BF16)
    lg = jnp.dot(jnp.concatenate([hi, lo_part], axis=0), wr_ref[...], preferred_element_type=F32)
    logits = lg[0:tm, 0:ne] + lg[0:tm, ne:2 * ne] + lg[tm:2 * tm, 0:ne]
    s = _sigmoid(jnp.transpose(jnp.concatenate([logits] * (LANES // ne), axis=1))[0:ne, :])
    sel = s + br_ref[...]
    a = [sel[j * N_GROUPS:(j + 1) * N_GROUPS, :] for j in range(EPG)]
    sj = [s[j * N_GROUPS:(j + 1) * N_GROUPS, :] for j in range(EPG)]
    m01, n01 = jnp.maximum(a[0], a[1]), jnp.minimum(a[0], a[1])
    m23, n23 = jnp.maximum(a[2], a[3]), jnp.minimum(a[2], a[3])
    gscore = jnp.maximum(m01, m23) + jnp.maximum(jnp.minimum(m01, m23), jnp.maximum(n01, n23))
    gi = lax.broadcasted_iota(jnp.int32, (N_GROUPS, tm), 0)
    gmax = jnp.max(gscore, axis=0, keepdims=True)
    g_idx = jnp.min(jnp.where(gscore == gmax, gi, N_GROUPS), axis=0, keepdims=True)
    g_hot = gi == g_idx
    cand = [jnp.sum(jnp.where(g_hot, a[j], 0.0), axis=0, keepdims=True) for j in range(EPG)]
    scand = [jnp.sum(jnp.where(g_hot, sj[j], 0.0), axis=0, keepdims=True) for j in range(EPG)]

    def _first_argmax(vals):
        best, idx = vals[0], jnp.zeros((1, tm), jnp.int32)
        for j in range(1, EPG):
            better = vals[j] > best
            idx = jnp.where(better, j, idx)
            best = jnp.where(better, vals[j], best)
        return idx

    i1 = _first_argmax(cand)
    i2 = _first_argmax([jnp.where(i1 == j, -jnp.inf, cand[j]) for j in range(EPG)])

    def _pick(vals, idx):
        out = vals[0]
        for j in range(1, EPG):
            out = jnp.where(idx == j, vals[j], out)
        return out

    s1, s2 = _pick(scand, i1), _pick(scand, i2)
    den = s1 + s2
    gate1, gate2 = s1 / den, s2 / den
    first_is_a = i1 < i2
    ea, eb = jnp.minimum(i1, i2), jnp.maximum(i1, i2)
    g_a = jnp.where(first_is_a, gate1, gate2)
    g_b = jnp.where(first_is_a, gate2, gate1)
    xr = d // LANES
    _slab_rows(hx_ref, xr, tm, x_rows)[...] = jnp.broadcast_to(_to_rows(g_a, tm), (tm, LANES))
    _slab_rows(hx_ref, xr + 1, tm, x_rows)[...] = jnp.broadcast_to(_to_rows(g_b, tm), (tm, LANES))
    for k in range(xr + 2, x_rows):
        _slab_rows(hx_ref, k, tm, x_rows)[...] = jnp.zeros((tm, LANES), F32)
    pair =jnp.zeros((1, tm), jnp.int32)
    for q in range(N_PAIRS):
        pair = jnp.where((ea == PAIR_A[q]) & (eb == PAIR_B[q]), q, pair)
    cls = g_idx * N_PAIRS + pair
    cls_ref[...] = cls

    ki = lax.broadcasted_iota(jnp.int32, (CLASS_ROWS, tm), 0)
    c_hot = ki == cls
    onehot = jnp.where(c_hot, 1.0, 0.0)
    tr = lax.broadcasted_iota(jnp.int32, (tm, tm), 0)
    tc = lax.broadcasted_iota(jnp.int32, (tm, tm), 1)
    upper = jnp.where(tr < tc, 1.0, 0.0).astype(BF16)
    prefix = jnp.dot(onehot.astype(BF16), upper, preferred_element_type=F32)
    base = cnt_scr[:, 0:1]
    rank = jnp.sum(jnp.where(c_hot, prefix + base, 0.0), axis=0, keepdims=True)
    rank_ref[...] = rank.astype(jnp.int32)
    cnt_scr[...] = cnt_scr[...] + jnp.sum(onehot, axis=-1, keepdims=True)
    cnt_ref[...] = cnt_scr[...]

    pa_scr[0:HALO_A, :] = pa_scr[tm:tm + HALO_A, :]
    glu_scr[:, 0:HALO_C, :] = glu_scr[:, tm:tm + HALO_C, :]


def _const_spec(shape, single_buffer=False):
    nd = len(shape)
    kw = {"pipeline_mode": pl.Buffered(1)} if single_buffer else {}
    return pl.BlockSpec(shape, lambda i, _nd=nd: (0,) * _nd, **kw)


def _layer_spec(arr, layer, single_buffer=False):
    nd = arr.ndim - 1
    kw = {"pipeline_mode": pl.Buffered(1)} if single_buffer else {}
    return pl.BlockSpec((None,) + arr.shape[1:], lambda i, _nd=nd: (layer,) + (0,) * _nd, **kw)


def _mixer_call(x, p, expert_w, layer, *, apply_ln_in, seq, alpha):
    t, d = x.shape
    tm = TM_MIX
    d_in = p["w_in"].shape[2]
    d_a, d_b, d_c = p["conv_a"].shape[2], p["ln_v_g"].shape[2], p["conv_c"].shape[2]
    heads, blk = p["gate_ws"].shape[1], p["gate_ws"].shape[2]
    assert t % tm == 0 and seq % tm == 0 and tm % blk == 0 and 3 * d_a + 2 * d_b + 2 * d_c == d_in
    assert d % LANES == 0 and p["conv_a"].shape[1] - 1 <= HALO_A and p["conv_c"].shape[1] - 1 <= HALO_C
    n_tiles = t // tm
    x_rows = d // LANES + SUBLANES
    assert x_rows % SUBLANES == 0 and d_c % LANES == 0
    kern = functools.partial(
        _mixer_kernel, apply_ln_in=apply_ln_in, tiles_per_seq=seq // tm, alpha=alpha,
        d_a=d_a, d_b=d_b, d_c=d_c, heads=heads, blk=blk, x_rows=x_rows)
    shared = [p["ln_in_g"], p["ln_in_b"]]
    per_layer = ["w_in", "conv_a", "gate_ws", "gate_bs_t", "ln_v_g", "ln_v_b", "conv_c", "conv_c_b",
                 "ln_c_g", "ln_c_b", "w_out", "ln1_g", "ln1_b"]
    router = [p["w_router"], p["b_router"]]
    in_specs = [pl.BlockSpec((tm, d), lambda i: (i, 0))]
    in_specs += [_const_spec(c.shape) for c in shared]
    in_specs += [_layer_spec(p[k], layer, single_buffer=k in ("w_in", "w_out")) for k in per_layer]
    in_specs += [_const_spec(c.shape) for c in router]
    experts = [w.reshape(w.shape[0], -1, w.shape[-1]) for w in expert_w]
    assert all(w.shape[1] % (n_tiles * 2 * SUBLANES) == 0 for w in experts)
    in_specs += [pl.BlockSpec((None, w.shape[1] // n_tiles, w.shape[2]), lambda i: (layer, i, 0))
                 for w in experts]
    tok_spec = pl.BlockSpec((1, tm), lambda i: (0, i))
    out_shape = (jax.ShapeDtypeStruct((t * x_rows, LANES), F32),
                 jax.ShapeDtypeStruct((1, t), jnp.int32),
                 jax.ShapeDtypeStruct((1, t), jnp.int32),
                 jax.ShapeDtypeStruct((CLASS_ROWS, LANES), F32))
    out_shape += tuple(jax.ShapeDtypeStruct(w.shape[1:], BF16) for w in experts)
    out_specs = (pl.BlockSpec((tm * x_rows, LANES), lambda i: (i, 0)),
                 tok_spec, tok_spec,
                 pl.BlockSpec((CLASS_ROWS, LANES), lambda i: (0, 0)))
    out_specs += tuple(pl.BlockSpec((w.shape[1] // n_tiles, w.shape[2]), lambda i: (i, 0))
                       for w in experts)
    scratch = [pltpu.VMEM((tm + HALO_A, d_a), F32),
               pltpu.VMEM((d_c // LANES, tm + HALO_C, LANES), F32),
               pltpu.VMEM((tm, d_a + d_b + d_c), BF16),
               pltpu.VMEM((CLASS_ROWS, LANES), F32)]
    return pl.pallas_call(
        kern, grid=(n_tiles,), in_specs=in_specs, out_specs=out_specs, out_shape=out_shape,
        scratch_shapes=scratch, name="mixer",
        compiler_params=pltpu.CompilerParams(
            dimension_semantics=("arbitrary",), vmem_limit_bytes=VMEM_LIMIT),
    )(x, *shared, *[p[k] for k in per_layer], *router, *experts)


def _start_slab_gather(idx_ref, src_hbm, buf, sem, slot, n, rows_per, dst_stride=None):
    dst_stride = rows_per if dst_stride is None else dst_stride

    def body(c, carry):
        for u in range(DMA_UNROLL):
            r = c * DMA_UNROLL + u
            src_row = pl.multiple_of(idx_ref[0, 0, r] * rows_per, SUBLANES)
            dst_row = pl.multiple_of(r * dst_stride, SUBLANES)
            pltpu.make_async_copy(src_hbm.at[pl.ds(src_row, rows_per), :],
                                  buf.at[slot, pl.ds(dst_row, rows_per), :],
                                  sem.at[slot]).start()
        return carry
    lax.fori_loop(0, n // DMA_UNROLL, body, 0)


def _wait_slab_gather(src_hbm, buf, sem, slot, n, rows_per):
    pltpu.make_async_copy(src_hbm.at[pl.ds(0, n * rows_per), :],
                          buf.at[slot, pl.ds(0, n * rows_per), :], sem.at[slot]).wait()


def _experts_kernel(ba_ref, bb_ref, nv_ref, idx0_ref, idx1_ref, idx2_ref, hx_hbm,
                    w1a_ref, w3a_ref, w2a_ref, w1b_ref, w3b_ref, w2b_ref,
                    y_ref, xbuf, sem, *, rows_per):
    d, f = w1a_ref.shape
    xr = d // LANES
    bm = y_ref.shape[0] // xr
    i = pl.program_id(0)
    nv = nv_ref[0]
    slot = i % GATHER_SLOTS

    @pl.when(i == 0)
    def _():
        _start_slab_gather(idx0_ref, hx_hbm, xbuf, sem, 0, bm, rows_per)

    @pl.when((i == 0) & (nv > 1))
    def _():
        _start_slab_gather(idx1_ref, hx_hbm, xbuf, sem, 1, bm, rows_per)

    @pl.when(i + 2 < nv)
    def _():
        _start_slab_gather(idx2_ref, hx_hbm, xbuf, sem, (i + 2) % GATHER_SLOTS, bm, rows_per)

    @pl.when(i < nv)
    def _():
        _wait_slab_gather(hx_hbm, xbuf, sem, slot, bm, rows_per)
        xb = jnp.concatenate(
            [_slab_rows(xbuf, s, bm, rows_per, slot)[...].astype(BF16) for s in range(xr)], axis=1)
        reps = f // LANES

        def hidden(w1_ref, w3_ref, gate):
            h1 = jnp.dot(xb, w1_ref[...], preferred_element_type=F32)
            h3 = jnp.dot(xb, w3_ref[...], preferred_element_type=F32)
            return (h1 * _sigmoid(h1) * h3 * jnp.concatenate([gate] * reps, axis=1)).astype(BF16)

        h_a = hidden(w1a_ref, w3a_ref, _slab_rows(xbuf, xr, bm, rows_per, slot)[...])
        h_b = hidden(w1b_ref, w3b_ref, _slab_rows(xbuf, xr + 1, bm, rows_per, slot)[...])
        y = (jnp.dot(h_a, w2a_ref[...], preferred_element_type=F32)
             + jnp.dot(h_b, w2b_ref[...], preferred_element_type=F32))
        for s in range(xr):
            _slab_rows(y_ref, s, bm, xr)[...] = y[:, s * LANES:(s + 1) * LANES]

    @pl.when(i >= nv)
    def _():
        y_ref[...] = jnp.zeros(y_ref.shape, F32)


def _experts_call(hx, slot_tok, blk_a, blk_b, nvalid, w1, w3, w2, d):
    bm = BM_MOE
    nb = slot_tok.shape[0]
    f = w1.shape[2]
    rows_per = d // LANES + SUBLANES
    y_rows = d // LANES
    assert f % LANES == 0

    def wspec(shape, which):
        if which == 0:
            return pl.BlockSpec(shape, lambda i, ba, bb, nv: (ba[i], 0, 0))
        return pl.BlockSpec(shape, lambda i, ba, bb, nv: (bb[i], 0, 0))

    def idx_spec(ahead):
        return pl.BlockSpec((1, 1, bm), lambda i, ba, bb, nv: (jnp.minimum(i + ahead, nb - 1), 0, 0),
                            memory_space=pltpu.SMEM)

    up, down = (None, d, f), (None, f, d)
    grid_spec = pltpu.PrefetchScalarGridSpec(
        num_scalar_prefetch=3,
        grid=(nb,),
        in_specs=[
            idx_spec(0), idx_spec(1), idx_spec(2),
            pl.BlockSpec(memory_space=pl.ANY),
            wspec(up, 0), wspec(up, 0), wspec(down, 0),
            wspec(up, 1), wspec(up, 1), wspec(down, 1),
        ],
        out_specs=pl.BlockSpec((bm * y_rows, LANES), lambda i, ba, bb, nv: (i, 0)),
        scratch_shapes=[pltpu.VMEM((GATHER_SLOTS, bm * rows_per, LANES), F32),
                        pltpu.SemaphoreType.DMA((GATHER_SLOTS,))],
    )
    return pl.pallas_call(
        functools.partial(_experts_kernel, rows_per=rows_per), grid_spec=grid_spec,
        out_shape=jax.ShapeDtypeStruct((nb * bm * y_rows, LANES), F32), name="experts",
        compiler_params=pltpu.CompilerParams(
            dimension_semantics=("arbitrary",), vmem_limit_bytes=VMEM_LIMIT),
    )(blk_a, blk_b, nvalid, slot_tok, slot_tok, slot_tok, hx, w1, w3, w2, w1, w3, w2)


def _combine_kernel(cur_idx_ref, nxt_idx_ref, y_hbm, hx_ref, g_ref, b_ref, o_ref, ybuf, sem,
                    *, alpha, x_rows):
    tm, d = o_ref.shape
    i = pl.program_id(0)
    n = pl.num_programs(0)
    slot = i % 2

    y_rows = d // LANES

    @pl.when(i == 0)
    def _():
        _start_slab_gather(cur_idx_ref, y_hbm, ybuf, sem, 0, tm, y_rows, x_rows)

    @pl.when(i + 1 < n)
    def _():
        _start_slab_gather(nxt_idx_ref, y_hbm, ybuf, sem, 1 - slot, tm, y_rows, x_rows)

    _wait_slab_gather(y_hbm, ybuf, sem, slot, tm, y_rows)
    r = jnp.concatenate(
        [alpha * _slab_rows(hx_ref, s, tm, x_rows)[...] + _slab_rows(ybuf, s, tm, x_rows, slot)[...]
         for s in range(y_rows)], axis=1)
    o_ref[...] = _layer_norm(r, g_ref[...], b_ref[...])


def _combine_call(y_sorted, dest, hx, g, b, layer, d, alpha):
    tm = TM_CMB
    rows_per = d // LANES + SUBLANES
    t = hx.shape[0] // rows_per
    n_tiles = t // tm
    ln_spec = pl.BlockSpec((None, 1, d), lambda i: (layer, 0, 0))
    return pl.pallas_call(
        functools.partial(_combine_kernel, alpha=alpha, x_rows=rows_per),
        grid=(n_tiles,),
        in_specs=[
            pl.BlockSpec((1, 1, tm), lambda i: (i, 0, 0), memory_space=pltpu.SMEM),
            pl.BlockSpec((1, 1, tm), lambda i: (jnp.minimum(i + 1, n_tiles - 1), 0, 0),
                         memory_space=pltpu.SMEM),
            pl.BlockSpec(memory_space=pl.ANY),
            pl.BlockSpec((tm * rows_per, LANES), lambda i: (i, 0)),
            ln_spec, ln_spec,
        ],
        out_specs=pl.BlockSpec((tm, d), lambda i: (i, 0)),
        out_shape=jax.ShapeDtypeStruct((t, d), F32),
        scratch_shapes=[pltpu.VMEM((2, tm * rows_per, LANES), F32), pltpu.SemaphoreType.DMA((2,))],
        name="combine",
        compiler_params=pltpu.CompilerParams(
            dimension_semantics=("arbitrary",), vmem_limit_bytes=VMEM_LIMIT),
    )(dest.reshape(n_tiles, 1, tm), dest.reshape(n_tiles, 1, tm), y_sorted, hx, g, b)


def _dispatch_plan(cls, rank, cnt, t):
    bm = BM_MOE
    nb = -(-t // bm) + N_CLASSES
    counts = cnt[:N_CLASSES, 0].astype(jnp.int32)
    nblk = (counts + bm - 1) // bm
    bend = jnp.cumsum(nblk)
    bstart = bend - nblk
    dest = bstart[cls] * bm + rank
    nvalid = bend[-1]
    blk = jnp.minimum(jnp.arange(nb, dtype=jnp.int32), nvalid - 1)
    blk_cls = jnp.sum((bend[None, :] <= blk[:, None]).astype(jnp.int32), axis=1)
    grp, pair = blk_cls // N_PAIRS, blk_cls % N_PAIRS
    blk_a = grp * EPG + jnp.asarray(PAIR_A, jnp.int32)[pair]
    blk_b = grp * EPG + jnp.asarray(PAIR_B, jnp.int32)[pair]
    slot_tok = jnp.zeros((nb * bm,), jnp.int32).at[dest].set(jnp.arange(t, dtype=jnp.int32))
    return dest, slot_tok.reshape(nb, 1, bm), blk_a, blk_b, nvalid.reshape(1)


def kernel(x, ln_in_g, ln_in_b, w_in, conv_a, gate_ws, gate_bs, ln_v_g, ln_v_b, conv_c, conv_c_b,
           ln_c_g, ln_c_b, w_out, ln1_g, ln1_b, w_router, b_router, w1, w3, w2, ln2_g, ln2_b):
    bsz, seq, d = x.shape
    depth = w_in.shape[0]
    t = bsz * seq
    alpha = float((2 * depth) ** 0.25)
    ne = N_GROUPS * EPG
    assert w_router.shape[1] == ne and w1.shape[1] == ne

    perm = np.array([g * EPG + j for j in range(EPG) for g in range(N_GROUPS)])
    wr_t = w_router.astype(F32).T[perm]
    wr_hi = wr_t.astype(BF16)
    wr_lo = (wr_t - wr_hi.astype(F32)).astype(BF16)

    rows = lambda v: v.reshape(v.shape[0], 1, -1).astype(F32)
    p = {
        "ln_in_g": ln_in_g.reshape(1, -1).astype(F32), "ln_in_b": ln_in_b.reshape(1, -1).astype(F32),
        "w_in": w_in.astype(BF16), "conv_a": conv_a, "gate_ws": gate_ws,
        "gate_bs_t": jnp.swapaxes(gate_bs, 1, 2), "ln_v_g": rows(ln_v_g), "ln_v_b": rows(ln_v_b),
        "conv_c": conv_c, "conv_c_b": rows(conv_c_b), "ln_c_g": rows(ln_c_g), "ln_c_b": rows(ln_c_b),
        "w_out": w_out.astype(BF16), "ln1_g": rows(ln1_g), "ln1_b": rows(ln1_b),
        "w_router": jnp.concatenate([wr_hi, wr_lo], axis=0).T,
        "b_router": b_router.astype(F32)[perm][:, None],
    }
    ln2g, ln2b = rows(ln2_g), rows(ln2_b)

    h = x.reshape(t, d)
    for l in range(depth):
        hx, cls, rank, cnt, w1b, w3b, w2b = _mixer_call(
            h, p, (w1, w3, w2), l, apply_ln_in=(l == 0), seq=seq, alpha=alpha)
        dest, slot_tok, blk_a, blk_b, nvalid = _dispatch_plan(cls[0], rank[0], cnt, t)
        y_sorted = _experts_call(hx, slot_tok, blk_a, blk_b, nvalid, w1b.reshape(w1.shape[1:]),
                                 w3b.reshape(w3.shape[1:]), w2b.reshape(w2.shape[1:]), d)
        h = _combine_call(y_sorted, dest, hx, ln2g, ln2b, l, d, alpha)
    return h.reshape(bsz, seq, d)
```

```python
import functools

import numpy as np
import jax
import jax.numpy as jnp
from jax import lax
from jax.experimental import pallas as pl
from jax.experimental.pallas import tpu as pltpu

F32 = jnp.float32
BF16 = jnp.bfloat16

LN_EPS = 1e-5
CHUNK = 64
N_GROUPS = 8
EPG = 4
PAIR_A = (0, 0, 0, 1, 1, 2)
PAIR_B = (1, 2, 3, 3, 2, 3)
N_PAIRS = len(PAIR_A)
N_CLASSES = N_GROUPS * N_PAIRS
CLASS_ROWS = 64
LANES = 128
SUBLANES = 8
HALO_A = 8
HALO_C = 32
VMEM_LIMIT = 60 * 1024 * 1024
DMA_UNROLL = 8
GATHER_SLOTS = 3

TM_MIX = 256
BM_MOE = 256
TM_CMB = 512


def _layer_norm(x, g, b):
    mu = jnp.mean(x, axis=-1, keepdims=True)
    xc = x - mu
    var = jnp.mean(xc * xc, axis=-1, keepdims=True)
    return xc * lax.rsqrt(var + LN_EPS) * g + b


def _sigmoid(x):
    return 1.0 / (1.0 + jnp.exp(-x))


def _to_rows(v, n):
    r = lax.broadcasted_iota(jnp.int32, (n, n), 0)
    c = lax.broadcasted_iota(jnp.int32, (n, n), 1)
    return jnp.sum(jnp.where(r == c, jnp.broadcast_to(v, (n, n)), 0.0), axis=-1, keepdims=True)


def _slab_rows(ref, k, n, rows_per, slot=None):
    win = pl.ds(k, n, stride=rows_per)
    return ref.at[win, :] if slot is None else ref.at[slot, win, :]


def _mixer_kernel(x_ref, lng_ref, lnb_ref, win_ref, conva_ref, ws_ref, gbt_ref, lnvg_ref, lnvb_ref,
                  convc_ref, convcb_ref, lncg_ref, lncb_ref, wout_ref, ln1g_ref, ln1b_ref,
                  wr_ref, br_ref, w1f_ref, w3f_ref, w2f_ref,
                  hx_ref, cls_ref, rank_ref, cnt_ref, w1b_ref, w3b_ref, w2b_ref,
                  pa_scr, glu_scr, ya_scr, yb_scr, yc_scr, cnt_scr,
                  *, apply_ln_in, tiles_per_seq, alpha, d_a, d_b, d_c, heads, blk, x_rows):
    tm, d = x_ref.shape
    hd = d_b // heads
    i = pl.program_id(0)

    w1b_ref[...] = w1f_ref[...].astype(BF16)
    w3b_ref[...] = w3f_ref[...].astype(BF16)
    w2b_ref[...] = w2f_ref[...].astype(BF16)

    @pl.when(i % tiles_per_seq == 0)
    def _():
        pa_scr[0:HALO_A, :] = jnp.zeros((HALO_A, d_a), F32)
        glu_scr[:, 0:HALO_C, :] = jnp.zeros((d_c // LANES, HALO_C, LANES), F32)

    @pl.when(i == 0)
    def _():
        cnt_scr[...] = jnp.zeros(cnt_scr.shape, F32)

    x = x_ref[...]
    if apply_ln_in:
        x = _layer_norm(x, lng_ref[...], lnb_ref[...])
    xb = x.astype(BF16)

    c0 = 3 * d_a + 2 * d_b
    h_c = jnp.dot(xb, win_ref[:, c0:c0 + 2 * d_c], preferred_element_type=F32)
    glu = h_c[:, 0:d_c] * _sigmoid(h_c[:, d_c:2 * d_c])
    kc = convc_ref.shape[0]
    parts = []
    for j in range(d_c // LANES):
        ln = slice(j * LANES, (j + 1) * LANES)
        glu_scr[j, HALO_C:HALO_C + tm, :] = glu[:, ln]
        acc_j = convcb_ref[:, ln] + convc_ref[kc - 1:kc, ln] * glu[:, ln]
        for k in range(kc - 1):
            off = HALO_C - (kc - 1) + k
            acc_j = acc_j + convc_ref[k:k + 1, ln] * glu_scr[j, off:off + tm, :]
        parts.append(acc_j)
    acc = jnp.concatenate(parts, axis=1)
    yc = _layer_norm(acc, lncg_ref[...], lncb_ref[...])
    yc_scr[...] = (yc * _sigmoid(yc)).astype(BF16)

    h_a = jnp.dot(xb, win_ref[:, 0:3 * d_a], preferred_element_type=F32)
    p = h_a[:, d_a:2 * d_a] * h_a[:, 2 * d_a:3 * d_a]
    pa_scr[HALO_A:HALO_A + tm, :] = p
    ka = conva_ref.shape[0]
    conv = conva_ref[ka - 1:ka, :] * p
    for k in range(ka - 1):
        off = HALO_A - (ka - 1) + k
        conv = conv + conva_ref[k:k + 1, :] * pa_scr[off:off + tm, :]
    ya_scr[...] = (h_a[:, 0:d_a] * conv).astype(BF16)

    c0 = 3 * d_a
    h_b = jnp.dot(xb, win_ref[:, c0:c0 + 2 * d_b], preferred_element_type=F32)
    u = h_b[:, 0:d_b]
    vn = _layer_norm(h_b[:, d_b:2 * d_b], lnvg_ref[...], lnvb_ref[...]).astype(BF16)
    ri = lax.broadcasted_iota(jnp.int32, (blk, blk), 0) // CHUNK
    ci = lax.broadcasted_iota(jnp.int32, (blk, blk), 1) // CHUNK
    for h in range(heads):
        ws_h = jnp.where(ci <= ri, ws_ref[h], 0.0).astype(BF16)
        bias_h = gbt_ref[:, h:h + 1]
        for n in range(tm // blk):
            br = slice(n * blk, (n + 1) * blk)
            cols = slice(h * hd, (h + 1) * hd)
            z = jnp.dot(ws_h, vn[br, cols], preferred_element_type=F32) + bias_h
            yb_scr[br, cols] = (u[br, cols] * z).astype(BF16)

    mix = (jnp.dot(yc_scr[...], wout_ref[d_a + d_b:d_a + d_b + d_c, :], preferred_element_type=F32)
           + jnp.dot(ya_scr[...], wout_ref[0:d_a, :], preferred_element_type=F32)
           + jnp.dot(yb_scr[...], wout_ref[d_a:d_a + d_b, :], preferred_element_type=F32))
    h1 = _layer_norm(alpha * x + mix, ln1g_ref[...], ln1b_ref[...])
    for s in range(d // LANES):
        _slab_rows(hx_ref, s, tm, x_rows)[...] = h1[:, s * LANES:(s + 1) * LANES]

    ne = N_GROUPS * EPG
    hi = h1.astype(BF16)
    lo_part = (h1 - hi.astype(F32)).astype(BF16)
    lg = jnp.dot(jnp.concatenate([hi, lo_part], axis=0), wr_ref[...], preferred_element_type=F32)
    logits = lg[0:tm, 0:ne] + lg[0:tm, ne:2 * ne] + lg[tm:2 * tm, 0:ne]
    s = _sigmoid(jnp.transpose(jnp.concatenate([logits] * (LANES // ne), axis=1))[0:ne, :])
    sel = s + br_ref[...]
    a = [sel[j * N_GROUPS:(j + 1) * N_GROUPS, :] for j in range(EPG)]
    sj = [s[j * N_GROUPS:(j + 1) * N_GROUPS, :] for j in range(EPG)]
    m01, n01 = jnp.maximum(a[0], a[1]), jnp.minimum(a[0], a[1])
    m23, n23 = jnp.maximum(a[2], a[3]), jnp.minimum(a[2], a[3])
    gscore = jnp.maximum(m01, m23) + jnp.maximum(jnp.minimum(m01, m23), jnp.maximum(n01, n23))
    gi = lax.broadcasted_iota(jnp.int32, (N_GROUPS, tm), 0)
    gmax = jnp.max(gscore, axis=0, keepdims=True)
    g_idx = jnp.min(jnp.where(gscore == gmax, gi, N_GROUPS), axis=0, keepdims=True)
    g_hot = gi == g_idx
    cand = [jnp.sum(jnp.where(g_hot, a[j], 0.0), axis=0, keepdims=True) for j in range(EPG)]
    scand = [jnp.sum(jnp.where(g_hot, sj[j], 0.0), axis=0, keepdims=True) for j in range(EPG)]

    def _first_argmax(vals):
        best, idx = vals[0], jnp.zeros((1, tm), jnp.int32)
        for j in range(1, EPG):
            better = vals[j] > best
            idx = jnp.where(better, j, idx)
            best = jnp.where(better, vals[j], best)
        return idx

    i1 = _first_argmax(cand)
    i2 = _first_argmax([jnp.where(i1 == j, -jnp.inf, cand[j]) for j in range(EPG)])

    def _pick(vals, idx):
        out = vals[0]
        for j in range(1, EPG):
            out = jnp.where(idx == j, vals[j], out)
        return out

    s1, s2 = _pick(scand, i1), _pick(scand, i2)
    den = s1 + s2
    gate1, gate2 = s1 / den, s2 / den
    first_is_a = i1 < i2
    ea, eb = jnp.minimum(i1, i2), jnp.maximum(i1, i2)
    g_a = jnp.where(first_is_a, gate1, gate2)
    g_b = jnp.where(first_is_a, gate2, gate1)
    xr = d // LANES
    _slab_rows(hx_ref, xr, tm, x_rows)[...] = jnp.broadcast_to(_to_rows(g_a, tm), (tm, LANES))
    _slab_rows(hx_ref, xr + 1, tm, x_rows)[...] = jnp.broadcast_to(_to_rows(g_b, tm), (tm, LANES))
    for k in range(xr + 2, x_rows):
        _slab_rows(hx_ref, k, tm, x_rows)[...] = jnp.zeros((tm, LANES), F32)
    pair =jnp.zeros((1, tm), jnp.int32)
    for q in range(N_PAIRS):
        pair = jnp.where((ea == PAIR_A[q]) & (eb == PAIR_B[q]), q, pair)
    cls = g_idx * N_PAIRS + pair
    cls_ref[...] = cls

    ki = lax.broadcasted_iota(jnp.int32, (CLASS_ROWS, tm), 0)
    c_hot = ki == cls
    onehot = jnp.where(c_hot, 1.0, 0.0)
    tr = lax.broadcasted_iota(jnp.int32, (tm, tm), 0)
    tc = lax.broadcasted_iota(jnp.int32, (tm, tm), 1)
    upper = jnp.where(tr < tc, 1.0, 0.0).astype(BF16)
    prefix = jnp.dot(onehot.astype(BF16), upper, preferred_element_type=F32)
    base = cnt_scr[:, 0:1]
    rank = jnp.sum(jnp.where(c_hot, prefix + base, 0.0), axis=0, keepdims=True)
    rank_ref[...] = rank.astype(jnp.int32)
    cnt_scr[...] = cnt_scr[...] + jnp.sum(onehot, axis=-1, keepdims=True)
    cnt_ref[...] = cnt_scr[...]

    pa_scr[0:HALO_A, :] = pa_scr[tm:tm + HALO_A, :]
    glu_scr[:, 0:HALO_C, :] = glu_scr[:, tm:tm + HALO_C, :]


def _const_spec(shape, single_buffer=False):
    nd = len(shape)
    kw = {"pipeline_mode": pl.Buffered(1)} if single_buffer else {}
    return pl.BlockSpec(shape, lambda i, _nd=nd: (0,) * _nd, **kw)


def _layer_spec(arr, layer, single_buffer=False):
    nd = arr.ndim - 1
    kw = {"pipeline_mode": pl.Buffered(1)} if single_buffer else {}
    return pl.BlockSpec((None,) + arr.shape[1:], lambda i, _nd=nd: (layer,) + (0,) * _nd, **kw)


def _mixer_call(x, p, expert_w, layer, *, apply_ln_in, seq, alpha):
    t, d = x.shape
    tm = TM_MIX
    d_in = p["w_in"].shape[2]
    d_a, d_b, d_c = p["conv_a"].shape[2], p["ln_v_g"].shape[2], p["conv_c"].shape[2]
    heads, blk = p["gate_ws"].shape[1], p["gate_ws"].shape[2]
    assert t % tm == 0 and seq % tm == 0 and tm % blk == 0 and 3 * d_a + 2 * d_b + 2 * d_c == d_in
    assert d % LANES == 0 and p["conv_a"].shape[1] - 1 <= HALO_A and p["conv_c"].shape[1] - 1 <= HALO_C
    n_tiles = t // tm
    x_rows = d // LANES + SUBLANES
    assert x_rows % SUBLANES == 0 and d_c % LANES == 0
    kern = functools.partial(
        _mixer_kernel, apply_ln_in=apply_ln_in, tiles_per_seq=seq // tm, alpha=alpha,
        d_a=d_a, d_b=d_b, d_c=d_c, heads=heads, blk=blk, x_rows=x_rows)
    shared = [p["ln_in_g"], p["ln_in_b"]]
    per_layer = ["w_in", "conv_a", "gate_ws", "gate_bs_t", "ln_v_g", "ln_v_b", "conv_c", "conv_c_b",
                 "ln_c_g", "ln_c_b", "w_out", "ln1_g", "ln1_b"]
    router = [p["w_router"], p["b_router"]]
    in_specs = [pl.BlockSpec((tm, d), lambda i: (i, 0))]
    in_specs += [_const_spec(c.shape) for c in shared]
    in_specs += [_layer_spec(p[k], layer, single_buffer=k in ("w_in", "w_out")) for k in per_layer]
    in_specs += [_const_spec(c.shape) for c in router]
    experts = [w.reshape(w.shape[0], -1, w.shape[-1]) for w in expert_w]
    assert all(w.shape[1] % (n_tiles * 2 * SUBLANES) == 0 for w in experts)
    in_specs += [pl.BlockSpec((None, w.shape[1] // n_tiles, w.shape[2]), lambda i: (layer, i, 0))
                 for w in experts]
    tok_spec = pl.BlockSpec((1, tm), lambda i: (0, i))
    out_shape = (jax.ShapeDtypeStruct((t * x_rows, LANES), F32),
                 jax.ShapeDtypeStruct((1, t), jnp.int32),
                 jax.ShapeDtypeStruct((1, t), jnp.int32),
                 jax.ShapeDtypeStruct((CLASS_ROWS, LANES), F32))
    out_shape += tuple(jax.ShapeDtypeStruct(w.shape[1:], BF16) for w in experts)
    out_specs = (pl.BlockSpec((tm * x_rows, LANES), lambda i: (i, 0)),
                 tok_spec, tok_spec,
                 pl.BlockSpec((CLASS_ROWS, LANES), lambda i: (0, 0)))
    out_specs += tuple(pl.BlockSpec((w.shape[1] // n_tiles, w.shape[2]), lambda i: (i, 0))
                       for w in experts)
    scratch = [pltpu.VMEM((tm + HALO_A, d_a), F32),
               pltpu.VMEM((d_c // LANES, tm + HALO_C, LANES), F32),
               pltpu.VMEM((tm, d_a), BF16),
               pltpu.VMEM((tm, d_b), BF16),
               pltpu.VMEM((tm, d_c), BF16),
               pltpu.VMEM((CLASS_ROWS, LANES), F32)]
    return pl.pallas_call(
        kern, grid=(n_tiles,), in_specs=in_specs, out_specs=out_specs, out_shape=out_shape,
        scratch_shapes=scratch, name="mixer",
        compiler_params=pltpu.CompilerParams(
            dimension_semantics=("arbitrary",), vmem_limit_bytes=VMEM_LIMIT),
    )(x, *shared, *[p[k] for k in per_layer], *router, *experts)


def _start_slab_gather(idx_ref, src_hbm, buf, sem, slot, n, rows_per, dst_stride=None):
    dst_stride = rows_per if dst_stride is None else dst_stride

    def body(c, carry):
        for u in range(DMA_UNROLL):
            r = c * DMA_UNROLL + u
            src_row = pl.multiple_of(idx_ref[0, 0, r] * rows_per, SUBLANES)
            dst_row = pl.multiple_of(r * dst_stride, SUBLANES)
            pltpu.make_async_copy(src_hbm.at[pl.ds(src_row, rows_per), :],
                                  buf.at[slot, pl.ds(dst_row, rows_per), :],
                                  sem.at[slot]).start()
        return carry
    lax.fori_loop(0, n // DMA_UNROLL, body, 0)


def _wait_slab_gather(src_hbm, buf, sem, slot, n, rows_per):
    pltpu.make_async_copy(src_hbm.at[pl.ds(0, n * rows_per), :],
                          buf.at[slot, pl.ds(0, n * rows_per), :], sem.at[slot]).wait()


def _experts_kernel(ba_ref, bb_ref, nv_ref, idx0_ref, idx1_ref, idx2_ref, hx_hbm,
                    w1a_ref, w3a_ref, w2a_ref, w1b_ref, w3b_ref, w2b_ref,
                    y_ref, xbuf, sem, *, rows_per):
    d, f = w1a_ref.shape
    xr = d // LANES
    bm = y_ref.shape[0] // xr
    i = pl.program_id(0)
    nv = nv_ref[0]
    slot = i % GATHER_SLOTS

    @pl.when(i == 0)
    def _():
        _start_slab_gather(idx0_ref, hx_hbm, xbuf, sem, 0, bm, rows_per)

    @pl.when((i == 0) & (nv > 1))
    def _():
        _start_slab_gather(idx1_ref, hx_hbm, xbuf, sem, 1, bm, rows_per)

    @pl.when(i + 2 < nv)
    def _():
        _start_slab_gather(idx2_ref, hx_hbm, xbuf, sem, (i + 2) % GATHER_SLOTS, bm, rows_per)

    @pl.when(i < nv)
    def _():
        _wait_slab_gather(hx_hbm, xbuf, sem, slot, bm, rows_per)
        xb = jnp.concatenate(
            [_slab_rows(xbuf, s, bm, rows_per, slot)[...].astype(BF16) for s in range(xr)], axis=1)
        reps = f // LANES

        def hidden(w1_ref, w3_ref, gate):
            h1 = jnp.dot(xb, w1_ref[...], preferred_element_type=F32)
            h3 = jnp.dot(xb, w3_ref[...], preferred_element_type=F32)
            return (h1 * _sigmoid(h1) * h3 * jnp.concatenate([gate] * reps, axis=1)).astype(BF16)

        h_a = hidden(w1a_ref, w3a_ref, _slab_rows(xbuf, xr, bm, rows_per, slot)[...])
        h_b = hidden(w1b_ref, w3b_ref, _slab_rows(xbuf, xr + 1, bm, rows_per, slot)[...])
        y = (jnp.dot(h_a, w2a_ref[...], preferred_element_type=F32)
             + jnp.dot(h_b, w2b_ref[...], preferred_element_type=F32))
        for s in range(xr):
            _slab_rows(y_ref, s, bm, xr)[...] = y[:, s * LANES:(s + 1) * LANES]

    @pl.when(i >= nv)
    def _():
        y_ref[...] = jnp.zeros(y_ref.shape, F32)


def _experts_call(hx, slot_tok, blk_a, blk_b, nvalid, w1, w3, w2, d):
    bm = BM_MOE
    nb = slot_tok.shape[0]
    f = w1.shape[2]
    rows_per = d // LANES + SUBLANES
    y_rows = d // LANES
    assert f % LANES == 0

    def wspec(shape, which):
        if which == 0:
            return pl.BlockSpec(shape, lambda i, ba, bb, nv: (ba[i], 0, 0))
        return pl.BlockSpec(shape, lambda i, ba, bb, nv: (bb[i], 0, 0))

    def idx_spec(ahead):
        return pl.BlockSpec((1, 1, bm), lambda i, ba, bb, nv: (jnp.minimum(i + ahead, nb - 1), 0, 0),
                            memory_space=pltpu.SMEM)

    up, down = (None, d, f), (None, f, d)
    grid_spec = pltpu.PrefetchScalarGridSpec(
        num_scalar_prefetch=3,
        grid=(nb,),
        in_specs=[
            idx_spec(0), idx_spec(1), idx_spec(2),
            pl.BlockSpec(memory_space=pl.ANY),
            wspec(up, 0), wspec(up, 0), wspec(down, 0),
            wspec(up, 1), wspec(up, 1), wspec(down, 1),
        ],
        out_specs=pl.BlockSpec((bm * y_rows, LANES), lambda i, ba, bb, nv: (i, 0)),
        scratch_shapes=[pltpu.VMEM((GATHER_SLOTS, bm * rows_per, LANES), F32),
                        pltpu.SemaphoreType.DMA((GATHER_SLOTS,))],
    )
    return pl.pallas_call(
        functools.partial(_experts_kernel, rows_per=rows_per), grid_spec=grid_spec,
        out_shape=jax.ShapeDtypeStruct((nb * bm * y_rows, LANES), F32), name="experts",
        compiler_params=pltpu.CompilerParams(
            dimension_semantics=("arbitrary",), vmem_limit_bytes=VMEM_LIMIT),
    )(blk_a, blk_b, nvalid, slot_tok, slot_tok, slot_tok, hx, w1, w3, w2, w1, w3, w2)


def _combine_kernel(cur_idx_ref, nxt_idx_ref, y_hbm, hx_ref, g_ref, b_ref, o_ref, ybuf, sem,
                    *, alpha, x_rows):
    tm, d = o_ref.shape
    i = pl.program_id(0)
    n = pl.num_programs(0)
    slot = i % 2

    y_rows = d // LANES

    @pl.when(i == 0)
    def _():
        _start_slab_gather(cur_idx_ref, y_hbm, ybuf, sem, 0, tm, y_rows, x_rows)

    @pl.when(i + 1 < n)
    def _():
        _start_slab_gather(nxt_idx_ref, y_hbm, ybuf, sem, 1 - slot, tm, y_rows, x_rows)

    _wait_slab_gather(y_hbm, ybuf, sem, slot, tm, y_rows)
    r = jnp.concatenate(
        [alpha * _slab_rows(hx_ref, s, tm, x_rows)[...] + _slab_rows(ybuf, s, tm, x_rows, slot)[...]
         for s in range(y_rows)], axis=1)
    o_ref[...] = _layer_norm(r, g_ref[...], b_ref[...])


def _combine_call(y_sorted, dest, hx, g, b, layer, d, alpha):
    tm = TM_CMB
    rows_per = d // LANES + SUBLANES
    t = hx.shape[0] // rows_per
    n_tiles = t // tm
    ln_spec = pl.BlockSpec((None, 1, d), lambda i: (layer, 0, 0))
    return pl.pallas_call(
        functools.partial(_combine_kernel, alpha=alpha, x_rows=rows_per),
        grid=(n_tiles,),
        in_specs=[
            pl.BlockSpec((1, 1, tm), lambda i: (i, 0, 0), memory_space=pltpu.SMEM),
            pl.BlockSpec((1, 1, tm), lambda i: (jnp.minimum(i + 1, n_tiles - 1), 0, 0),
                         memory_space=pltpu.SMEM),
            pl.BlockSpec(memory_space=pl.ANY),
            pl.BlockSpec((tm * rows_per, LANES), lambda i: (i, 0)),
            ln_spec, ln_spec,
        ],
        out_specs=pl.BlockSpec((tm, d), lambda i: (i, 0)),
        out_shape=jax.ShapeDtypeStruct((t, d), F32),
        scratch_shapes=[pltpu.VMEM((2, tm * rows_per, LANES), F32), pltpu.SemaphoreType.DMA((2,))],
        name="combine",
        compiler_params=pltpu.CompilerParams(
            dimension_semantics=("arbitrary",), vmem_limit_bytes=VMEM_LIMIT),
    )(dest.reshape(n_tiles, 1, tm), dest.reshape(n_tiles, 1, tm), y_sorted, hx, g, b)


def _dest_kernel(off_ref, cls_ref, rank_ref, dest_ref):
    cls = cls_ref[...]
    off = jnp.zeros(cls.shape, jnp.int32)
    for c in range(N_CLASSES):
        off = jnp.where(cls == c, off_ref[c], off)
    dest_ref[...] = rank_ref[...] + off


def _dest_call(cls, rank, class_row0):
    spec = pl.BlockSpec(cls.shape, lambda i, off: (0, 0))
    return pl.pallas_call(
        _dest_kernel,
        grid_spec=pltpu.PrefetchScalarGridSpec(
            num_scalar_prefetch=1, grid=(1,), in_specs=[spec, spec], out_specs=spec),
        out_shape=jax.ShapeDtypeStruct(cls.shape, jnp.int32), name="dest",
    )(class_row0, cls, rank)


def _dispatch_plan(cls, rank, cnt, t):
    bm = BM_MOE
    nb = -(-t // bm) + N_CLASSES
    counts = cnt[:N_CLASSES, 0].astype(jnp.int32)
    nblk = (counts + bm - 1) // bm
    bend = jnp.cumsum(nblk)
    bstart = bend - nblk
    dest = _dest_call(cls, rank, bstart * bm)[0]
    nvalid = bend[-1]
    blk = jnp.minimum(jnp.arange(nb, dtype=jnp.int32), nvalid - 1)
    blk_cls = jnp.sum((bend[None, :] <= blk[:, None]).astype(jnp.int32), axis=1)
    grp, pair = blk_cls // N_PAIRS, blk_cls % N_PAIRS
    blk_a = grp * EPG + jnp.asarray(PAIR_A, jnp.int32)[pair]
    blk_b = grp * EPG + jnp.asarray(PAIR_B, jnp.int32)[pair]
    slot_tok = jnp.zeros((nb * bm,), jnp.int32).at[dest].set(
        jnp.arange(t, dtype=jnp.int32), unique_indices=True, mode="promise_in_bounds")
    return dest, slot_tok.reshape(nb, 1, bm), blk_a, blk_b, nvalid.reshape(1)


def kernel(x, ln_in_g, ln_in_b, w_in, conv_a, gate_ws, gate_bs, ln_v_g, ln_v_b, conv_c, conv_c_b,
           ln_c_g, ln_c_b, w_out, ln1_g, ln1_b, w_router, b_router, w1, w3, w2, ln2_g, ln2_b):
    bsz, seq, d = x.shape
    depth = w_in.shape[0]
    t = bsz * seq
    alpha = float((2 * depth) ** 0.25)
    ne = N_GROUPS * EPG
    assert w_router.shape[1] == ne and w1.shape[1] == ne

    perm = np.array([g * EPG + j for j in range(EPG) for g in range(N_GROUPS)])
    wr_t = w_router.astype(F32).T[perm]
    wr_hi = wr_t.astype(BF16)
    wr_lo = (wr_t - wr_hi.astype(F32)).astype(BF16)

    rows = lambda v: v.reshape(v.shape[0], 1, -1).astype(F32)
    p = {
        "ln_in_g": ln_in_g.reshape(1, -1).astype(F32), "ln_in_b": ln_in_b.reshape(1, -1).astype(F32),
        "w_in": w_in.astype(BF16), "conv_a": conv_a, "gate_ws": gate_ws,
        "gate_bs_t": jnp.swapaxes(gate_bs, 1, 2), "ln_v_g": rows(ln_v_g), "ln_v_b": rows(ln_v_b),
        "conv_c": conv_c, "conv_c_b": rows(conv_c_b), "ln_c_g": rows(ln_c_g), "ln_c_b": rows(ln_c_b),
        "w_out": w_out.astype(BF16), "ln1_g": rows(ln1_g), "ln1_b": rows(ln1_b),
        "w_router": jnp.concatenate([wr_hi, wr_lo], axis=0).T,
        "b_router": b_router.astype(F32)[perm][:, None],
    }
    ln2g, ln2b = rows(ln2_g), rows(ln2_b)

    h = x.reshape(t, d)
    for l in range(depth):
        hx, cls, rank, cnt, w1b, w3b, w2b = _mixer_call(
            h, p, (w1, w3, w2), l, apply_ln_in=(l == 0), seq=seq, alpha=alpha)
        dest, slot_tok, blk_a, blk_b, nvalid = _dispatch_plan(cls, rank, cnt, t)
        y_sorted = _experts_call(hx, slot_tok, blk_a, blk_b, nvalid, w1b.reshape(w1.shape[1:]),
                                 w3b.reshape(w3.shape[1:]), w2b.reshape(w2.shape[1:]), d)
        h = _combine_call(y_sorted, dest, hx, ln2g, ln2b, l, d, alpha)
    return h.reshape(bsz, seq, d)
```

```python
import functools

import numpy as np
import jax
import jax.numpy as jnp
from jax import lax
from jax.experimental import pallas as pl
from jax.experimental.pallas import tpu as pltpu

F32 = jnp.float32
BF16 = jnp.bfloat16

LN_EPS = 1e-5
CHUNK = 64
N_GROUPS = 8
EPG = 4
PAIR_A = (0, 0, 0, 1, 1, 2)
PAIR_B = (1, 2, 3, 3, 2, 3)
N_PAIRS = len(PAIR_A)
N_CLASSES = N_GROUPS * N_PAIRS
CLASS_ROWS = 64
LANES = 128
SUBLANES = 8
HALO_A = 8
HALO_C = 32
VMEM_LIMIT = 60 * 1024 * 1024
DMA_UNROLL = 16
GATHER_SLOTS = 3

TM_MIX = 256
BM_MOE = 256
TM_CMB = 512


def _layer_norm(x, g, b):
    mu = jnp.mean(x, axis=-1, keepdims=True)
    xc = x - mu
    var = jnp.mean(xc * xc, axis=-1, keepdims=True)
    return xc * lax.rsqrt(var + LN_EPS) * g + b


def _sigmoid(x):
    return 1.0 / (1.0 + jnp.exp(-x))


def _to_rows(v, n):
    r = lax.broadcasted_iota(jnp.int32, (n, n), 0)
    c = lax.broadcasted_iota(jnp.int32, (n, n), 1)
    return jnp.sum(jnp.where(r == c, jnp.broadcast_to(v, (n, n)), 0.0), axis=-1, keepdims=True)


def _slab_rows(ref, k, n, rows_per, slot=None):
    win = pl.ds(k, n, stride=rows_per)
    return ref.at[win, :] if slot is None else ref.at[slot, win, :]


def _mixer_kernel(x_ref, lng_ref, lnb_ref, win_ref, conva_ref, ws_ref, gbt_ref, lnvg_ref, lnvb_ref,
                  convc_ref, convcb_ref, lncg_ref, lncb_ref, wout_ref, ln1g_ref, ln1b_ref,
                  wr_ref, br_ref, w1f_ref, w3f_ref, w2f_ref,
                  hx_ref, cls_ref, rank_ref, cnt_ref, w1b_ref, w3b_ref, w2b_ref,
                  pa_scr, glu_scr, ya_scr, yb_scr, yc_scr, cnt_scr,
                  *, apply_ln_in, tiles_per_seq, alpha, d_a, d_b, d_c, heads, blk, x_rows):
    tm, d = x_ref.shape
    hd = d_b // heads
    i = pl.program_id(0)

    w1b_ref[...] = w1f_ref[...].astype(BF16)
    w3b_ref[...] = w3f_ref[...].astype(BF16)
    w2b_ref[...] = w2f_ref[...].astype(BF16)

    @pl.when(i % tiles_per_seq == 0)
    def _():
        pa_scr[0:HALO_A, :] = jnp.zeros((HALO_A, d_a), F32)
        glu_scr[:, 0:HALO_C, :] = jnp.zeros((d_c // LANES, HALO_C, LANES), F32)

    @pl.when(i == 0)
    def _():
        cnt_scr[...] = jnp.zeros(cnt_scr.shape, F32)

    x = x_ref[...]
    if apply_ln_in:
        x = _layer_norm(x, lng_ref[...], lnb_ref[...])
    xb = x.astype(BF16)

    c0 = 3 * d_a + 2 * d_b
    h_c = jnp.dot(xb, win_ref[:, c0:c0 + 2 * d_c], preferred_element_type=F32)
    glu = h_c[:, 0:d_c] * _sigmoid(h_c[:, d_c:2 * d_c])
    kc = convc_ref.shape[0]
    parts = []
    for j in range(d_c // LANES):
        ln = slice(j * LANES, (j + 1) * LANES)
        glu_scr[j, HALO_C:HALO_C + tm, :] = glu[:, ln]
        acc_j = convcb_ref[:, ln] + convc_ref[kc - 1:kc, ln] * glu[:, ln]
        for k in range(kc - 1):
            off = HALO_C - (kc - 1) + k
            acc_j = acc_j + convc_ref[k:k + 1, ln] * glu_scr[j, off:off + tm, :]
        parts.append(acc_j)
    acc = jnp.concatenate(parts, axis=1)
    yc = _layer_norm(acc, lncg_ref[...], lncb_ref[...])
    yc_scr[...] = (yc * _sigmoid(yc)).astype(BF16)

    h_a = jnp.dot(xb, win_ref[:, 0:3 * d_a], preferred_element_type=F32)
    p = h_a[:, d_a:2 * d_a] * h_a[:, 2 * d_a:3 * d_a]
    pa_scr[HALO_A:HALO_A + tm, :] = p
    ka = conva_ref.shape[0]
    conv = conva_ref[ka - 1:ka, :] * p
    for k in range(ka - 1):
        off = HALO_A - (ka - 1) + k
        conv = conv + conva_ref[k:k + 1, :] * pa_scr[off:off + tm, :]
    ya_scr[...] = (h_a[:, 0:d_a] * conv).astype(BF16)

    c0 = 3 * d_a
    h_b = jnp.dot(xb, win_ref[:, c0:c0 + 2 * d_b], preferred_element_type=F32)
    u = h_b[:, 0:d_b]
    vn = _layer_norm(h_b[:, d_b:2 * d_b], lnvg_ref[...], lnvb_ref[...]).astype(BF16)
    ri = lax.broadcasted_iota(jnp.int32, (blk, blk), 0) // CHUNK
    ci = lax.broadcasted_iota(jnp.int32, (blk, blk), 1) // CHUNK
    for h in range(heads):
        ws_h = jnp.where(ci <= ri, ws_ref[h], 0.0).astype(BF16)
        bias_h = gbt_ref[:, h:h + 1]
        for n in range(tm // blk):
            br = slice(n * blk, (n + 1) * blk)
            cols = slice(h * hd, (h + 1) * hd)
            z = jnp.dot(ws_h, vn[br, cols], preferred_element_type=F32) + bias_h
            yb_scr[br, cols] = (u[br, cols] * z).astype(BF16)

    mix = (jnp.dot(yc_scr[...], wout_ref[d_a + d_b:d_a + d_b + d_c, :], preferred_element_type=F32)
           + jnp.dot(ya_scr[...], wout_ref[0:d_a, :], preferred_element_type=F32)
           + jnp.dot(yb_scr[...], wout_ref[d_a:d_a + d_b, :], preferred_element_type=F32))
    h1 = _layer_norm(alpha * x + mix, ln1g_ref[...], ln1b_ref[...])
    for s in range(d // LANES):
        _slab_rows(hx_ref, s, tm, x_rows)[...] = h1[:, s * LANES:(s + 1) * LANES]

    ne = N_GROUPS * EPG
    hi = h1.astype(BF16)
    lo_part = (h1 - hi.astype(F32)).astype(BF16)
    lg = jnp.dot(jnp.concatenate([hi, lo_part], axis=0), wr_ref[...], preferred_element_type=F32)
    logits = lg[0:tm, 0:ne] + lg[0:tm, ne:2 * ne] + lg[tm:2 * tm, 0:ne]
    s = _sigmoid(jnp.transpose(jnp.concatenate([logits] * (LANES // ne), axis=1))[0:ne, :])
    sel = s + br_ref[...]
    a = [sel[j * N_GROUPS:(j + 1) * N_GROUPS, :] for j in range(EPG)]
    sj = [s[j * N_GROUPS:(j + 1) * N_GROUPS, :] for j in range(EPG)]
    m01, n01 = jnp.maximum(a[0], a[1]), jnp.minimum(a[0], a[1])
    m23, n23 = jnp.maximum(a[2], a[3]), jnp.minimum(a[2], a[3])
    gscore = jnp.maximum(m01, m23) + jnp.maximum(jnp.minimum(m01, m23), jnp.maximum(n01, n23))
    gi = lax.broadcasted_iota(jnp.int32, (N_GROUPS, tm), 0)
    gmax = jnp.max(gscore, axis=0, keepdims=True)
    g_idx = jnp.min(jnp.where(gscore == gmax, gi, N_GROUPS), axis=0, keepdims=True)
    g_hot = gi == g_idx
    cand = [jnp.sum(jnp.where(g_hot, a[j], 0.0), axis=0, keepdims=True) for j in range(EPG)]
    scand = [jnp.sum(jnp.where(g_hot, sj[j], 0.0), axis=0, keepdims=True) for j in range(EPG)]

    def _first_argmax(vals):
        best, idx = vals[0], jnp.zeros((1, tm), jnp.int32)
        for j in range(1, EPG):
            better = vals[j] > best
            idx = jnp.where(better, j, idx)
            best = jnp.where(better, vals[j], best)
        return idx

    i1 = _first_argmax(cand)
    i2 = _first_argmax([jnp.where(i1 == j, -jnp.inf, cand[j]) for j in range(EPG)])

    def _pick(vals, idx):
        out = vals[0]
        for j in range(1, EPG):
            out = jnp.where(idx == j, vals[j], out)
        return out

    s1, s2 = _pick(scand, i1), _pick(scand, i2)
    den = s1 + s2
    gate1, gate2 = s1 / den, s2 / den
    first_is_a = i1 < i2
    ea, eb = jnp.minimum(i1, i2), jnp.maximum(i1, i2)
    g_a = jnp.where(first_is_a, gate1, gate2)
    g_b = jnp.where(first_is_a, gate2, gate1)
    xr = d // LANES
    _slab_rows(hx_ref, xr, tm, x_rows)[...] = jnp.broadcast_to(_to_rows(g_a, tm), (tm, LANES))
    _slab_rows(hx_ref, xr + 1, tm, x_rows)[...] = jnp.broadcast_to(_to_rows(g_b, tm), (tm, LANES))
    for k in range(xr + 2, x_rows):
        _slab_rows(hx_ref, k, tm, x_rows)[...] = jnp.zeros((tm, LANES), F32)
    pair =jnp.zeros((1, tm), jnp.int32)
    for q in range(N_PAIRS):
        pair = jnp.where((ea == PAIR_A[q]) & (eb == PAIR_B[q]), q, pair)
    cls = g_idx * N_PAIRS + pair
    cls_ref[...] = cls

    ki = lax.broadcasted_iota(jnp.int32, (CLASS_ROWS, tm), 0)
    c_hot = ki == cls
    onehot = jnp.where(c_hot, 1.0, 0.0)
    tr = lax.broadcasted_iota(jnp.int32, (tm, tm), 0)
    tc = lax.broadcasted_iota(jnp.int32, (tm, tm), 1)
    upper = jnp.where(tr < tc, 1.0, 0.0).astype(BF16)
    prefix = jnp.dot(onehot.astype(BF16), upper, preferred_element_type=F32)
    base = cnt_scr[:, 0:1]
    rank = jnp.sum(jnp.where(c_hot, prefix + base, 0.0), axis=0, keepdims=True)
    rank_ref[...] = rank.astype(jnp.int32)
    cnt_scr[...] = cnt_scr[...] + jnp.sum(onehot, axis=-1, keepdims=True)
    cnt_ref[...] = cnt_scr[...]

    pa_scr[0:HALO_A, :] = pa_scr[tm:tm + HALO_A, :]
    glu_scr[:, 0:HALO_C, :] = glu_scr[:, tm:tm + HALO_C, :]


def _const_spec(shape, single_buffer=False):
    nd = len(shape)
    kw = {"pipeline_mode": pl.Buffered(1)} if single_buffer else {}
    return pl.BlockSpec(shape, lambda i, _nd=nd: (0,) * _nd, **kw)


def _layer_spec(arr, layer, single_buffer=False):
    nd = arr.ndim - 1
    kw = {"pipeline_mode": pl.Buffered(1)} if single_buffer else {}
    return pl.BlockSpec((None,) + arr.shape[1:], lambda i, _nd=nd: (layer,) + (0,) * _nd, **kw)


def _mixer_call(x, p, expert_w, layer, *, apply_ln_in, seq, alpha):
    t, d = x.shape
    tm = TM_MIX
    d_in = p["w_in"].shape[2]
    d_a, d_b, d_c = p["conv_a"].shape[2], p["ln_v_g"].shape[2], p["conv_c"].shape[2]
    heads, blk = p["gate_ws"].shape[1], p["gate_ws"].shape[2]
    assert t % tm == 0 and seq % tm == 0 and tm % blk == 0 and 3 * d_a + 2 * d_b + 2 * d_c == d_in
    assert d % LANES == 0 and p["conv_a"].shape[1] - 1 <= HALO_A and p["conv_c"].shape[1] - 1 <= HALO_C
    n_tiles = t // tm
    x_rows = d // LANES + SUBLANES
    assert x_rows % SUBLANES == 0 and d_c % LANES == 0
    kern = functools.partial(
        _mixer_kernel, apply_ln_in=apply_ln_in, tiles_per_seq=seq // tm, alpha=alpha,
        d_a=d_a, d_b=d_b, d_c=d_c, heads=heads, blk=blk, x_rows=x_rows)
    shared = [p["ln_in_g"], p["ln_in_b"]]
    per_layer = ["w_in", "conv_a", "gate_ws", "gate_bs_t", "ln_v_g", "ln_v_b", "conv_c", "conv_c_b",
                 "ln_c_g", "ln_c_b", "w_out", "ln1_g", "ln1_b"]
    router = [p["w_router"], p["b_router"]]
    in_specs = [pl.BlockSpec((tm, d), lambda i: (i, 0))]
    in_specs += [_const_spec(c.shape) for c in shared]
    in_specs += [_layer_spec(p[k], layer, single_buffer=k in ("w_in", "w_out")) for k in per_layer]
    in_specs += [_const_spec(c.shape) for c in router]
    experts = [w.reshape(w.shape[0], -1, w.shape[-1]) for w in expert_w]
    assert all(w.shape[1] % (n_tiles * 2 * SUBLANES) == 0 for w in experts)
    in_specs += [pl.BlockSpec((None, w.shape[1] // n_tiles, w.shape[2]), lambda i: (layer, i, 0))
                 for w in experts]
    tok_spec = pl.BlockSpec((1, tm), lambda i: (0, i))
    out_shape = (jax.ShapeDtypeStruct((t * x_rows, LANES), F32),
                 jax.ShapeDtypeStruct((1, t), jnp.int32),
                 jax.ShapeDtypeStruct((1, t), jnp.int32),
                 jax.ShapeDtypeStruct((CLASS_ROWS, LANES), F32))
    out_shape += tuple(jax.ShapeDtypeStruct(w.shape[1:], BF16) for w in experts)
    out_specs = (pl.BlockSpec((tm * x_rows, LANES), lambda i: (i, 0)),
                 tok_spec, tok_spec,
                 pl.BlockSpec((CLASS_ROWS, LANES), lambda i: (0, 0)))
    out_specs += tuple(pl.BlockSpec((w.shape[1] // n_tiles, w.shape[2]), lambda i: (i, 0))
                       for w in experts)
    scratch = [pltpu.VMEM((tm + HALO_A, d_a), F32),
               pltpu.VMEM((d_c // LANES, tm + HALO_C, LANES), F32),
               pltpu.VMEM((tm, d_a), BF16),
               pltpu.VMEM((tm, d_b), BF16),
               pltpu.VMEM((tm, d_c), BF16),
               pltpu.VMEM((CLASS_ROWS, LANES), F32)]
    return pl.pallas_call(
        kern, grid=(n_tiles,), in_specs=in_specs, out_specs=out_specs, out_shape=out_shape,
        scratch_shapes=scratch, name="mixer",
        compiler_params=pltpu.CompilerParams(
            dimension_semantics=("arbitrary",), vmem_limit_bytes=VMEM_LIMIT),
    )(x, *shared, *[p[k] for k in per_layer], *router, *experts)


def _start_slab_gather(idx_ref, src_hbm, buf, sem, slot, n, rows_per, dst_stride=None):
    dst_stride = rows_per if dst_stride is None else dst_stride

    def body(c, carry):
        for u in range(DMA_UNROLL):
            r = c * DMA_UNROLL + u
            src_row = pl.multiple_of(idx_ref[0, 0, r] * rows_per, SUBLANES)
            dst_row = pl.multiple_of(r * dst_stride, SUBLANES)
            pltpu.make_async_copy(src_hbm.at[pl.ds(src_row, rows_per), :],
                                  buf.at[slot, pl.ds(dst_row, rows_per), :],
                                  sem.at[slot]).start()
        return carry
    lax.fori_loop(0, n // DMA_UNROLL, body, 0)


def _wait_slab_gather(src_hbm, buf, sem, slot, n, rows_per):
    pltpu.make_async_copy(src_hbm.at[pl.ds(0, n * rows_per), :],
                          buf.at[slot, pl.ds(0, n * rows_per), :], sem.at[slot]).wait()


def _experts_kernel(ba_ref, bb_ref, nv_ref, idx0_ref, idx1_ref, idx2_ref, hx_hbm,
                    w1a_ref, w3a_ref, w2a_ref, w1b_ref, w3b_ref, w2b_ref,
                    y_ref, xbuf, sem, *, rows_per, alpha):
    d, f = w1a_ref.shape
    xr = d // LANES
    bm = y_ref.shape[0] // xr
    i = pl.program_id(0)
    nv = nv_ref[0]
    slot = i % GATHER_SLOTS

    @pl.when(i == 0)
    def _():
        _start_slab_gather(idx0_ref, hx_hbm, xbuf, sem, 0, bm, rows_per)

    @pl.when((i == 0) & (nv > 1))
    def _():
        _start_slab_gather(idx1_ref, hx_hbm, xbuf, sem, 1, bm, rows_per)

    @pl.when(i + 2 < nv)
    def _():
        _start_slab_gather(idx2_ref, hx_hbm, xbuf, sem, (i + 2) % GATHER_SLOTS, bm, rows_per)

    @pl.when(i < nv)
    def _():
        _wait_slab_gather(hx_hbm, xbuf, sem, slot, bm, rows_per)
        xb = jnp.concatenate(
            [_slab_rows(xbuf, s, bm, rows_per, slot)[...].astype(BF16) for s in range(xr)], axis=1)
        reps = f // LANES

        def hidden(w1_ref, w3_ref, gate):
            h1 = jnp.dot(xb, w1_ref[...], preferred_element_type=F32)
            h3 = jnp.dot(xb, w3_ref[...], preferred_element_type=F32)
            return (h1 * _sigmoid(h1) * h3 * jnp.concatenate([gate] * reps, axis=1)).astype(BF16)

        h_a = hidden(w1a_ref, w3a_ref, _slab_rows(xbuf, xr, bm, rows_per, slot)[...])
        h_b = hidden(w1b_ref, w3b_ref, _slab_rows(xbuf, xr + 1, bm, rows_per, slot)[...])
        y = (jnp.dot(h_a, w2a_ref[...], preferred_element_type=F32)
             + jnp.dot(h_b, w2b_ref[...], preferred_element_type=F32))
        for s in range(xr):
            _slab_rows(y_ref, s, bm, xr)[...] = (
                alpha * _slab_rows(xbuf, s, bm, rows_per, slot)[...] + y[:, s * LANES:(s + 1) * LANES])

    @pl.when(i >= nv)
    def _():
        y_ref[...] = jnp.zeros(y_ref.shape, F32)


def _experts_call(hx, slot_tok, blk_a, blk_b, nvalid, w1, w3, w2, d, alpha):
    bm = BM_MOE
    nb = slot_tok.shape[0]
    f = w1.shape[2]
    rows_per = d // LANES + SUBLANES
    y_rows = d // LANES
    assert f % LANES == 0

    def wspec(shape, which):
        if which == 0:
            return pl.BlockSpec(shape, lambda i, ba, bb, nv: (ba[i], 0, 0))
        return pl.BlockSpec(shape, lambda i, ba, bb, nv: (bb[i], 0, 0))

    def idx_spec(ahead):
        return pl.BlockSpec((1, 1, bm), lambda i, ba, bb, nv: (jnp.minimum(i + ahead, nb - 1), 0, 0),
                            memory_space=pltpu.SMEM)

    up, down = (None, d, f), (None, f, d)
    grid_spec = pltpu.PrefetchScalarGridSpec(
        num_scalar_prefetch=3,
        grid=(nb,),
        in_specs=[
            idx_spec(0), idx_spec(1), idx_spec(2),
            pl.BlockSpec(memory_space=pl.ANY),
            wspec(up, 0), wspec(up, 0), wspec(down, 0),
            wspec(up, 1), wspec(up, 1), wspec(down, 1),
        ],
        out_specs=pl.BlockSpec((bm * y_rows, LANES), lambda i, ba, bb, nv: (i, 0)),
        scratch_shapes=[pltpu.VMEM((GATHER_SLOTS, bm * rows_per, LANES), F32),
                        pltpu.SemaphoreType.DMA((GATHER_SLOTS,))],
    )
    return pl.pallas_call(
        functools.partial(_experts_kernel, rows_per=rows_per, alpha=alpha), grid_spec=grid_spec,
        out_shape=jax.ShapeDtypeStruct((nb * bm * y_rows, LANES), F32), name="experts",
        compiler_params=pltpu.CompilerParams(
            dimension_semantics=("arbitrary",), vmem_limit_bytes=VMEM_LIMIT),
    )(blk_a, blk_b, nvalid, slot_tok, slot_tok, slot_tok, hx, w1, w3, w2, w1, w3, w2)


def _combine_kernel(cur_idx_ref, nxt_idx_ref, y_hbm, g_ref, b_ref, o_ref, ybuf, sem, *, x_rows):
    tm, d = o_ref.shape
    i = pl.program_id(0)
    n = pl.num_programs(0)
    slot = i % 2

    y_rows = d // LANES

    @pl.when(i == 0)
    def _():
        _start_slab_gather(cur_idx_ref, y_hbm, ybuf, sem, 0, tm, y_rows, x_rows)

    @pl.when(i + 1 < n)
    def _():
        _start_slab_gather(nxt_idx_ref, y_hbm, ybuf, sem, 1 - slot, tm, y_rows, x_rows)

    _wait_slab_gather(y_hbm, ybuf, sem, slot, tm, y_rows)
    r = jnp.concatenate(
        [_slab_rows(ybuf, s, tm, x_rows, slot)[...] for s in range(y_rows)], axis=1)
    o_ref[...] = _layer_norm(r, g_ref[...], b_ref[...])


def _combine_call(y_sorted, dest, g, b, layer, d):
    tm = TM_CMB
    rows_per = d // LANES + SUBLANES
    t = dest.shape[0]
    n_tiles = t // tm
    ln_spec = pl.BlockSpec((None, 1, d), lambda i: (layer, 0, 0))
    return pl.pallas_call(
        functools.partial(_combine_kernel, x_rows=rows_per),
        grid=(n_tiles,),
        in_specs=[
            pl.BlockSpec((1, 1, tm), lambda i: (i, 0, 0), memory_space=pltpu.SMEM),
            pl.BlockSpec((1, 1, tm), lambda i: (jnp.minimum(i + 1, n_tiles - 1), 0, 0),
                         memory_space=pltpu.SMEM),
            pl.BlockSpec(memory_space=pl.ANY),
            ln_spec, ln_spec,
        ],
        out_specs=pl.BlockSpec((tm, d), lambda i: (i, 0)),
        out_shape=jax.ShapeDtypeStruct((t, d), F32),
        scratch_shapes=[pltpu.VMEM((2, tm * rows_per, LANES), F32), pltpu.SemaphoreType.DMA((2,))],
        name="combine",
        compiler_params=pltpu.CompilerParams(
            dimension_semantics=("arbitrary",), vmem_limit_bytes=VMEM_LIMIT),
    )(dest.reshape(n_tiles, 1, tm), dest.reshape(n_tiles, 1, tm), y_sorted, g, b)


def _dest_kernel(off_ref, cls_ref, rank_ref, dest_ref):
    cls = cls_ref[...]
    off = jnp.zeros(cls.shape, jnp.int32)
    for c in range(N_CLASSES):
        off = jnp.where(cls == c, off_ref[c], off)
    dest_ref[...] = rank_ref[...] + off


def _dest_call(cls, rank, class_row0):
    spec = pl.BlockSpec(cls.shape, lambda i, off: (0, 0))
    return pl.pallas_call(
        _dest_kernel,
        grid_spec=pltpu.PrefetchScalarGridSpec(
            num_scalar_prefetch=1, grid=(1,), in_specs=[spec, spec], out_specs=spec),
        out_shape=jax.ShapeDtypeStruct(cls.shape, jnp.int32), name="dest",
    )(class_row0, cls, rank)


def _dispatch_plan(cls, rank, cnt, t):
    bm = BM_MOE
    nb = -(-t // bm) + N_CLASSES
    counts = cnt[:N_CLASSES, 0].astype(jnp.int32)
    nblk = (counts + bm - 1) // bm
    bend = jnp.cumsum(nblk)
    bstart = bend - nblk
    dest = _dest_call(cls, rank, bstart * bm)[0]
    nvalid = bend[-1]
    blk = jnp.minimum(jnp.arange(nb, dtype=jnp.int32), nvalid - 1)
    blk_cls = jnp.sum((bend[None, :] <= blk[:, None]).astype(jnp.int32), axis=1)
    grp, pair = blk_cls // N_PAIRS, blk_cls % N_PAIRS
    blk_a = grp * EPG + jnp.asarray(PAIR_A, jnp.int32)[pair]
    blk_b = grp * EPG + jnp.asarray(PAIR_B, jnp.int32)[pair]
    slot_tok = jnp.zeros((nb * bm,), jnp.int32).at[dest].set(
        jnp.arange(t, dtype=jnp.int32), unique_indices=True, mode="promise_in_bounds")
    return dest, slot_tok.reshape(nb, 1, bm), blk_a, blk_b, nvalid.reshape(1)


def kernel(x, ln_in_g, ln_in_b, w_in, conv_a, gate_ws, gate_bs, ln_v_g, ln_v_b, conv_c, conv_c_b,
           ln_c_g, ln_c_b, w_out, ln1_g, ln1_b, w_router, b_router, w1, w3, w2, ln2_g, ln2_b):
    bsz, seq, d = x.shape
    depth = w_in.shape[0]
    t = bsz * seq
    alpha = float((2 * depth) ** 0.25)
    ne = N_GROUPS * EPG
    assert w_router.shape[1] == ne and w1.shape[1] == ne

    perm = np.array([g * EPG + j for j in range(EPG) for g in range(N_GROUPS)])
    wr_t = w_router.astype(F32).T[perm]
    wr_hi = wr_t.astype(BF16)
    wr_lo = (wr_t - wr_hi.astype(F32)).astype(BF16)

    rows = lambda v: v.reshape(v.shape[0], 1, -1).astype(F32)
    p = {
        "ln_in_g": ln_in_g.reshape(1, -1).astype(F32), "ln_in_b": ln_in_b.reshape(1, -1).astype(F32),
        "w_in": w_in.astype(BF16), "conv_a": conv_a, "gate_ws": gate_ws,
        "gate_bs_t": jnp.swapaxes(gate_bs, 1, 2), "ln_v_g": rows(ln_v_g), "ln_v_b": rows(ln_v_b),
        "conv_c": conv_c, "conv_c_b": rows(conv_c_b), "ln_c_g": rows(ln_c_g), "ln_c_b": rows(ln_c_b),
        "w_out": w_out.astype(BF16), "ln1_g": rows(ln1_g), "ln1_b": rows(ln1_b),
        "w_router": jnp.concatenate([wr_hi, wr_lo], axis=0).T,
        "b_router": b_router.astype(F32)[perm][:, None],
    }
    ln2g, ln2b = rows(ln2_g), rows(ln2_b)

    h = x.reshape(t, d)
    for l in range(depth):
        hx, cls, rank, cnt, w1b, w3b, w2b = _mixer_call(
            h, p, (w1, w3, w2), l, apply_ln_in=(l == 0), seq=seq, alpha=alpha)
        dest, slot_tok, blk_a, blk_b, nvalid = _dispatch_plan(cls, rank, cnt, t)
        y_sorted = _experts_call(hx, slot_tok, blk_a, blk_b, nvalid, w1b.reshape(w1.shape[1:]),
                                 w3b.reshape(w3.shape[1:]), w2b.reshape(w2.shape[1:]), d, alpha)
        h = _combine_call(y_sorted, dest, ln2g, ln2b, l, d)
    return h.reshape(bsz, seq, d)
```
